```python
import math, functools
import jax, jax.numpy as jnp
from jax import lax
import numpy as np

D_MODEL = 1024
BATCH = 8
SEQ = 4096
DEPTH = 1
DEC_BATCH = 32
DEC_SEQ = 8
PAST_LEN = 16384
PAGE_SIZE = 128

SSM_EXPAND = 2
D_INNER = SSM_EXPAND * D_MODEL
SSM_HEAD_DIM = 64
SSM_HEADS = D_INNER // SSM_HEAD_DIM
SSM_GROUPS = 4
HEADS_PER_GROUP = SSM_HEADS // SSM_GROUPS
SSM_STATE = 128
CONV_WIDTH = 4
CONV_DIM = D_INNER + 2 * SSM_GROUPS * SSM_STATE
SSD_CHUNK = 256
DT_MIN = 0.001
DT_MAX = 0.1
ATT_HEADS = 16
ATT_HEAD_DIM = 64
D_ATT = ATT_HEADS * ATT_HEAD_DIM
ROPE_DIM = ATT_HEAD_DIM // 4
ROPE_THETA = 500000.0
MOBA_BLOCK = 256
MOBA_TOPK = 3
Q_CHUNK = 32
PAGES_PER_BLOCK = MOBA_BLOCK // PAGE_SIZE
D_FF = -(-8 * D_MODEL // (3 * 256)) * 256
RMS_EPS = 1e-6
OFF_Z = 0
OFF_XBC = OFF_Z + D_INNER
OFF_DT = OFF_XBC + CONV_DIM
OFF_Q = OFF_DT + SSM_HEADS
OFF_K = OFF_Q + D_ATT
OFF_V = OFF_K + D_ATT
OFF_G = OFF_V + D_ATT
IN_DIM = OFF_G + 2 * D_MODEL
F32 = jnp.float32

kernel_name = 'hybrid_ssd_moba_decoder_step'


def rms_norm(x, w):
    xf = x.astype(F32)
    y = xf * lax.rsqrt(jnp.mean(xf * xf, axis=-1, keepdims=True) + RMS_EPS)
    return (y * w.astype(F32)).astype(x.dtype)


def rope_partial(x, pos):
    half = ROPE_DIM // 2
    inv_freq = ROPE_THETA ** (-(jnp.arange(half, dtype=F32) * 2.0 / ROPE_DIM))
    ang = pos.astype(F32)[:, None] * inv_freq[None, :]
    cos = jnp.cos(ang)[None, :, None, :]
    sin = jnp.sin(ang)[None, :, None, :]
    xf = x.astype(F32)
    x1 = xf[..., :half]
    x2 = xf[..., half:ROPE_DIM]
    out = jnp.concatenate([x1 * cos - x2 * sin, x2 * cos + x1 * sin, xf[..., ROPE_DIM:]], axis=-1)
    return out.astype(x.dtype)


def causal_conv(xbc, conv_prev, conv_w, conv_b):
    full = jnp.concatenate([conv_prev.astype(xbc.dtype), xbc], axis=1)
    out = lax.conv_general_dilated(full, conv_w[:, None, :].astype(xbc.dtype), window_strides=(1,),
                                   padding='VALID', dimension_numbers=('NWC', 'WIO', 'NWC'),
                                   feature_group_count=CONV_DIM)
    return jax.nn.silu(out + conv_b.astype(xbc.dtype)), full[:, -(CONV_WIDTH - 1):]


def ssd_chunked_scan(x, dt, a, bmat, cmat, h0):
    b, T = x.shape[:2]
    L = min(SSD_CHUNK, T)
    pad = (-T) % L

    def prep(u):
        u = jnp.pad(u, [(0, 0), (0, pad)] + [(0, 0)] * (u.ndim - 2))
        return u.reshape((b, -1, L) + u.shape[2:]).swapaxes(0, 1)

    causal = jnp.tril(jnp.ones((L, L), dtype=bool))[None, :, :, None, None]

    def step(h, inp):
        xc, dtc, bc, cc = inp
        acum = jnp.cumsum(dtc * a, axis=1)
        decay = jnp.exp(jnp.where(causal, acum[:, :, None] - acum[:, None], -jnp.inf))
        xdt = xc * dtc[..., None]
        cb = jnp.einsum('blgn,bsgn->blsg', cc, bc)
        y = jnp.einsum('blsgr,bsgrp->blgrp', cb[..., None] * decay, xdt)
        y = y + jnp.einsum('blgn,bgrpn->blgrp', cc, h) * jnp.exp(acum)[..., None]
        last = acum[:, -1]
        h = h * jnp.exp(last)[..., None, None] + jnp.einsum(
            'bsgrp,bsgn->bgrpn', xdt * jnp.exp(last[:, None] - acum)[..., None], bc)
        return h, y

    h, ys = lax.scan(step, h0, (prep(x), prep(dt), prep(bmat), prep(cmat)))
    y = ys.swapaxes(0, 1).reshape((b, -1) + x.shape[2:])[:, :T]
    return y, h


def ssd_mixer(z, xbc_raw, dt_raw, conv_prev, ssm_prev, conv_w, conv_b, dt_bias, a_log, d_skip, ssm_norm_w):
    b, T, _ = z.shape
    xbc, conv_new = causal_conv(xbc_raw, conv_prev, conv_w, conv_b)
    xbc = xbc.astype(F32)
    gn = SSM_GROUPS * SSM_STATE
    xs = xbc[..., :D_INNER].reshape(b, T, SSM_GROUPS, HEADS_PER_GROUP, SSM_HEAD_DIM)
    bmat = xbc[..., D_INNER:D_INNER + gn].reshape(b, T, SSM_GROUPS, SSM_STATE)
    cmat = xbc[..., D_INNER + gn:].reshape(b, T, SSM_GROUPS, SSM_STATE)
    dt = jax.nn.softplus(dt_raw.astype(F32) + dt_bias.astype(F32)).reshape(b, T, SSM_GROUPS, HEADS_PER_GROUP)
    a = -jnp.exp(a_log.astype(F32)).reshape(SSM_GROUPS, HEADS_PER_GROUP)
    h0 = ssm_prev.astype(F32).reshape(b, SSM_GROUPS, HEADS_PER_GROUP, SSM_HEAD_DIM, SSM_STATE)
    y, h = ssd_chunked_scan(xs, dt, a, bmat, cmat, h0)
    y = y + d_skip.astype(F32).reshape(SSM_GROUPS, HEADS_PER_GROUP)[..., None] * xs
    y = y.reshape(b, T, D_INNER) * jax.nn.silu(z.astype(F32))
    y = rms_norm(y.reshape(b, T, SSM_GROUPS, -1), ssm_norm_w.reshape(SSM_GROUPS, -1)).reshape(b, T, D_INNER)
    return y.astype(z.dtype), conv_new, h.reshape(b, SSM_HEADS, SSM_HEAD_DIM, SSM_STATE)


def moba_prompt(q, k, v):
    b, S, H, hd = q.shape
    nb = -(-S // MOBA_BLOCK)
    pad = nb * MOBA_BLOCK - S

    def blocks(u):
        u = jnp.pad(u, ((0, 0), (0, pad), (0, 0), (0, 0)))
        return u.reshape(b, nb, MOBA_BLOCK, H, hd).transpose(0, 3, 1, 2, 4)

    kb, vb = blocks(k), blocks(v)
    kmean = jnp.mean(kb.astype(F32), axis=3)
    qt = q.transpose(0, 2, 1, 3)
    k_sel = min(MOBA_TOPK, nb)
    n_sel = k_sel * MOBA_BLOCK
    scale = ATT_HEAD_DIM ** -0.5
    bi = jnp.arange(b)[:, None, None, None]
    hi = jnp.arange(H)[None, :, None, None]

    def query_chunk(c):
        start = c * Q_CHUNK
        qc = lax.dynamic_slice_in_dim(qt, start, Q_CHUNK, axis=2)
        pos = start + jnp.arange(Q_CHUNK)
        blk = start // MOBA_BLOCK
        gate = jnp.einsum('bhqd,bhnd->bhqn', qc.astype(F32), kmean)
        gate = jnp.where(jnp.arange(nb) < blk, gate, -jnp.inf)
        _, idx = lax.top_k(gate, k_sel)
        ksel = kb[bi, hi, idx]
        vsel = vb[bi, hi, idx]
        s_sel = jnp.einsum('bhqd,bhqkjd->bhqkj', qc, ksel, preferred_element_type=F32) * scale
        s_sel = jnp.where((jnp.arange(k_sel) < blk)[:, None], s_sel, -jnp.inf)
        kown = lax.dynamic_index_in_dim(kb, blk, axis=2, keepdims=False)
        vown = lax.dynamic_index_in_dim(vb, blk, axis=2, keepdims=False)
        s_own = jnp.einsum('bhqd,bhjd->bhqj', qc, kown, preferred_element_type=F32) * scale
        own_pos = blk * MOBA_BLOCK + jnp.arange(MOBA_BLOCK)
        s_own = jnp.where(own_pos[None, :] <= pos[:, None], s_own, -jnp.inf)
        p = jax.nn.softmax(jnp.concatenate([s_sel.reshape(b, H, Q_CHUNK, n_sel), s_own], axis=-1), axis=-1)
        p_sel = p[..., :n_sel].reshape(b, H, Q_CHUNK, k_sel, MOBA_BLOCK).astype(v.dtype)
        p_own = p[..., n_sel:].astype(v.dtype)
        out = (jnp.einsum('bhqkj,bhqkjd->bhqd', p_sel, vsel, preferred_element_type=F32)
               + jnp.einsum('bhqj,bhjd->bhqd', p_own, vown, preferred_element_type=F32))
        return out.astype(q.dtype)

    outs = lax.map(query_chunk, jnp.arange(S // Q_CHUNK))
    return outs.transpose(1, 0, 3, 2, 4).reshape(b, S, H, hd)


def moba_sample(q, k_new, v_new, cache_k, cache_v, page_table, layer):
    b, T, H, hd = q.shape
    n_full = PAST_LEN // MOBA_BLOCK
    own_first_page = n_full * PAGES_PER_BLOCK
    n_own_pages = PAST_LEN // PAGE_SIZE - own_first_page
    scale = ATT_HEAD_DIM ** -0.5
    qt = q.transpose(0, 2, 1, 3)
    score_parts = []
    k_sel = 0
    if n_full > 0:
        def block_means(pt):
            rows = cache_k[layer, pt[:own_first_page]]
            return jnp.mean(rows.reshape(n_full, MOBA_BLOCK, H, hd).astype(F32), axis=1)
        kmean = lax.map(block_means, page_table)
        gate = jnp.einsum('bhtd,bnhd->bhtn', qt.astype(F32), kmean)
        k_sel = min(MOBA_TOPK, n_full)
        _, idx = lax.top_k(gate, k_sel)
        logical = idx[..., None] * PAGES_PER_BLOCK + jnp.arange(PAGES_PER_BLOCK)
        phys = page_table[jnp.arange(b)[:, None, None, None, None], logical]
        hi = jnp.arange(H)[None, :, None, None, None, None]
        slot = jnp.arange(PAGE_SIZE)
        ksel = cache_k[layer, phys[..., None], slot, hi].reshape(b, H, T, k_sel * MOBA_BLOCK, hd)
        vsel = cache_v[layer, phys[..., None], slot, hi].reshape(b, H, T, k_sel * MOBA_BLOCK, hd)
        score_parts.append(jnp.einsum('bhtd,bhtjd->bhtj', qt, ksel.astype(q.dtype), preferred_element_type=F32) * scale)
    k_sh, v_sh = k_new, v_new
    if n_own_pages > 0:
        phys_own = page_table[:, own_first_page:own_first_page + n_own_pages]
        k_own = cache_k[layer, phys_own].reshape(b, n_own_pages * PAGE_SIZE, H, hd).astype(k_new.dtype)
        v_own = cache_v[layer, phys_own].reshape(b, n_own_pages * PAGE_SIZE, H, hd).astype(v_new.dtype)
        k_sh = jnp.concatenate([k_own, k_new], axis=1)
        v_sh = jnp.concatenate([v_own, v_new], axis=1)
    n_prev = n_own_pages * PAGE_SIZE
    s_sh = jnp.einsum('bhtd,bjhd->bhtj', qt, k_sh, preferred_element_type=F32) * scale
    vis = jnp.arange(n_prev + T)[None, :] <= (n_prev + jnp.arange(T))[:, None]
    score_parts.append(jnp.where(vis, s_sh, -jnp.inf))
    p = jax.nn.softmax(jnp.concatenate(score_parts, axis=-1), axis=-1)
    n_sel = k_sel * MOBA_BLOCK
    out = jnp.einsum('bhtj,bjhd->bhtd', p[..., n_sel:].astype(v_sh.dtype), v_sh, preferred_element_type=F32)
    if n_full > 0:
        out = out + jnp.einsum('bhtj,bhtjd->bhtd', p[..., :n_sel].astype(v_new.dtype), vsel.astype(v_new.dtype),
                               preferred_element_type=F32)
    return out.transpose(0, 2, 1, 3).astype(q.dtype)


def trunk_layer(x, pos, conv_prev, ssm_prev, attend, norm1_w, w_in, conv_w, conv_b, dt_bias, a_log, d_skip,
                ssm_norm_w, q_norm_w, k_norm_w, w_ssm_branch, w_att_branch, w_out, norm2_w, w_ffn_in, w_ffn_out):
    b, t, _ = x.shape
    h = rms_norm(x, norm1_w)
    proj = jnp.einsum('btd,de->bte', h, w_in)
    y_ssm, conv_new, ssm_new = ssd_mixer(proj[..., OFF_Z:OFF_XBC], proj[..., OFF_XBC:OFF_DT],
                                         proj[..., OFF_DT:OFF_Q], conv_prev, ssm_prev, conv_w, conv_b,
                                         dt_bias, a_log, d_skip, ssm_norm_w)
    heads = (b, t, ATT_HEADS, ATT_HEAD_DIM)
    q = rope_partial(rms_norm(proj[..., OFF_Q:OFF_K].reshape(heads), q_norm_w), pos)
    k = rope_partial(rms_norm(proj[..., OFF_K:OFF_V].reshape(heads), k_norm_w), pos)
    v = proj[..., OFF_V:OFF_G].reshape(heads)
    y_att = attend(q, k, v).reshape(b, t, D_ATT)
    g = jax.nn.sigmoid(proj[..., OFF_G:].astype(F32))
    u_ssm = jnp.einsum('bte,ed->btd', y_ssm, w_ssm_branch).astype(F32)
    u_att = jnp.einsum('bte,ed->btd', y_att, w_att_branch).astype(F32)
    merged = (g[..., :D_MODEL] * u_ssm + g[..., D_MODEL:] * u_att).astype(x.dtype)
    x = x + jnp.einsum('btd,de->bte', merged, w_out)
    gu = jnp.einsum('btd,df->btf', rms_norm(x, norm2_w), w_ffn_in)
    x = x + jnp.einsum('btf,fd->btd', jax.nn.silu(gu[..., :D_FF]) * gu[..., D_FF:], w_ffn_out)
    return x, k, v, conv_new, ssm_new


def setup_inputs(seed: int = 0) -> dict:
    key = jax.random.key(seed)
    ks = jax.random.split(key, 23)
    n_pages = PAST_LEN // PAGE_SIZE
    n_used = DEC_BATCH * n_pages
    n_pool = n_used + max(1, n_used // 4)

    def nrm(k, shape, scale=1.0):
        return jax.random.normal(k, shape, F32) * scale

    def gain(k, shape):
        return 1.0 + 0.02 * jax.random.normal(k, shape, F32)

    page_table = jax.random.permutation(ks[4], n_pool)[:n_used].reshape(DEC_BATCH, n_pages).astype(jnp.int32)
    dt0 = jnp.exp(jax.random.uniform(ks[10], (DEPTH, SSM_HEADS), F32, math.log(DT_MIN), math.log(DT_MAX)))
    dt_bias = dt0 + jnp.log(-jnp.expm1(-dt0))
    kv_shape = (DEPTH, n_pool, PAGE_SIZE, ATT_HEADS, ATT_HEAD_DIM)
    return {
        'x_prompt': nrm(ks[0], (BATCH, SEQ, D_MODEL)),
        'x_sample': nrm(ks[1], (DEC_BATCH, DEC_SEQ, D_MODEL)),
        'cache_k': nrm(ks[2], kv_shape),
        'cache_v': nrm(ks[3], kv_shape),
        'page_table': page_table,
        'state_conv': nrm(ks[5], (DEPTH, DEC_BATCH, CONV_WIDTH - 1, CONV_DIM)),
        'state_ssm': nrm(ks[6], (DEPTH, DEC_BATCH, SSM_HEADS, SSM_HEAD_DIM, SSM_STATE), 0.1),
        'norm1_w': gain(ks[7], (DEPTH, D_MODEL)),
        'w_in': nrm(ks[8], (DEPTH, D_MODEL, IN_DIM), D_MODEL ** -0.5),
        'conv_w': nrm(ks[9], (DEPTH, CONV_WIDTH, CONV_DIM), CONV_WIDTH ** -0.5),
        'conv_b': nrm(ks[11], (DEPTH, CONV_DIM), 0.02),
        'dt_bias': dt_bias,
        'a_log': jnp.log(jax.random.uniform(ks[12], (DEPTH, SSM_HEADS), F32, 1.0, 16.0)),
        'd_skip': gain(ks[13], (DEPTH, SSM_HEADS)),
        'ssm_norm_w': gain(ks[14], (DEPTH, D_INNER)),
        'q_norm_w': gain(ks[15], (DEPTH, ATT_HEAD_DIM)),
        'k_norm_w': gain(ks[16], (DEPTH, ATT_HEAD_DIM)),
        'w_ssm_branch': nrm(ks[17], (DEPTH, D_INNER, D_MODEL), D_INNER ** -0.5),
        'w_att_branch': nrm(ks[18], (DEPTH, D_ATT, D_MODEL), D_ATT ** -0.5),
        'w_out': nrm(ks[19], (DEPTH, D_MODEL, D_MODEL), D_MODEL ** -0.5),
        'norm2_w': gain(ks[20], (DEPTH, D_MODEL)),
        'w_ffn_in': nrm(ks[21], (DEPTH, D_MODEL, 2 * D_FF), D_MODEL ** -0.5),
        'w_ffn_out': nrm(ks[22], (DEPTH, D_FF, D_MODEL), D_FF ** -0.5),
    }


def reference(x_prompt, x_sample, cache_k, cache_v, page_table, state_conv, state_ssm, norm1_w, w_in, conv_w,
              conv_b, dt_bias, a_log, d_skip, ssm_norm_w, q_norm_w, k_norm_w, w_ssm_branch, w_att_branch, w_out,
              norm2_w, w_ffn_in, w_ffn_out):
    pos_p = jnp.arange(x_prompt.shape[1], dtype=jnp.int32)
    pos_s = PAST_LEN + jnp.arange(x_sample.shape[1], dtype=jnp.int32)
    xp, xs = x_prompt, x_sample
    kp_l, vp_l, ks_l, vs_l, cp_l, cs_l, sp_l, ss_l = [], [], [], [], [], [], [], []
    for l in range(DEPTH):
        wts = (norm1_w[l], w_in[l], conv_w[l], conv_b[l], dt_bias[l], a_log[l], d_skip[l], ssm_norm_w[l],
               q_norm_w[l], k_norm_w[l], w_ssm_branch[l], w_att_branch[l], w_out[l], norm2_w[l], w_ffn_in[l],
               w_ffn_out[l])
        conv0 = jnp.zeros((xp.shape[0], CONV_WIDTH - 1, CONV_DIM), xp.dtype)
        ssm0 = jnp.zeros((xp.shape[0], SSM_HEADS, SSM_HEAD_DIM, SSM_STATE), F32)
        xp, kp, vp, cp, sp = trunk_layer(xp, pos_p, conv0, ssm0, moba_prompt, *wts)
        attend_s = functools.partial(moba_sample, cache_k=cache_k, cache_v=cache_v, page_table=page_table, layer=l)
        xs, ksn, vsn, cs, ss = trunk_layer(xs, pos_s, state_conv[l], state_ssm[l], attend_s, *wts)
        kp_l.append(kp); vp_l.append(vp); ks_l.append(ksn); vs_l.append(vsn)
        cp_l.append(cp); cs_l.append(cs); sp_l.append(sp); ss_l.append(ss)
    return (xp, xs, jnp.stack(kp_l), jnp.stack(vp_l), jnp.stack(ks_l), jnp.stack(vs_l),
            jnp.stack(cp_l), jnp.stack(cs_l), jnp.stack(sp_l), jnp.stack(ss_l))
```

```python
import functools

import jax
import jax.numpy as jnp
from jax import lax
from jax.experimental import pallas as pl
from jax.experimental.pallas import tpu as pltpu

F32 = jnp.float32
BF16 = jnp.bfloat16

D_MODEL = 1024
D_INNER = 2048
SSM_HEADS = 32
SSM_HEAD_DIM = 64
SSM_GROUPS = 4
SSM_STATE = 128
CONV_WIDTH = 4
CONV_DIM = D_INNER + 2 * SSM_GROUPS * SSM_STATE
SSD_CHUNK = 256
ATT_HEADS = 16
ATT_HEAD_DIM = 64
D_ATT = ATT_HEADS * ATT_HEAD_DIM
ROPE_DIM = 16
ROPE_HALF = ROPE_DIM // 2
ROPE_THETA = 500000.0
MOBA_BLOCK = 256
MOBA_TOPK = 3
PAGE_SIZE = 128
PAGES_PER_BLOCK = MOBA_BLOCK // PAGE_SIZE
D_FF = 2816
RMS_EPS = 1e-6
NEG_INF = float("-inf")

LANES = 128
SUBLANES = 8
V7X_VMEM_BYTES = 64 * 1024 * 1024

COL_XBC = 0
COL_Q = COL_XBC + CONV_DIM
COL_Z = COL_Q + D_ATT
COL_G = COL_Z + D_INNER
COL_K = COL_G + 2 * D_MODEL
PROJ_COLS = COL_K + D_ATT
PROJ_TILE = 1024
DT_PAD = LANES


def _vmem_limit(nbytes):
    return int(min(nbytes, V7X_VMEM_BYTES - 4 * 1024 * 1024))


def _dot(a, b):
    return jnp.dot(a, b, preferred_element_type=F32)


def _dot_nt(a, b):
    return lax.dot_general(a, b, (((1,), (1,)), ((), ())), preferred_element_type=F32)


def _dot_tn(a, b):
    return lax.dot_general(a, b, (((0,), (0,)), ((), ())), preferred_element_type=F32)


def _split2(x):
    hi = x.astype(BF16)
    lo = (x - hi.astype(F32)).astype(BF16)
    return hi, lo


def _split3(x):
    hi = x.astype(BF16)
    r = x - hi.astype(F32)
    mid = r.astype(BF16)
    lo = (r - mid.astype(F32)).astype(BF16)
    return hi, mid, lo


def _dot_f32(a, b):
    a_hi, a_lo = _split2(a)
    b_hi, b_lo = _split2(b)
    return _dot(a_hi, b_hi) + _dot(a_hi, b_lo) + _dot(a_lo, b_hi)


def _sigmoid(x):
    return 1.0 / (1.0 + jnp.exp(-x))


def _softplus(x):
    return jnp.maximum(x, 0.0) + jnp.log1p(jnp.exp(-jnp.abs(x)))


def _in_proj_kernel(x_ref, nw_ref, w_ref, wdt_hi_ref, wdt_lo_ref, proj_ref, v_ref, dt_ref, h_ref, *, n_col):
    j = pl.program_id(1)

    @pl.when(j == 0)
    def _():
        x = x_ref[...]
        ms = jnp.mean(x * x, axis=-1, keepdims=True)
        h = x * lax.rsqrt(ms + RMS_EPS) * nw_ref[...]
        h_hi, h_lo = _split2(h)
        h_ref[...] = h_hi
        dt_ref[...] = (_dot(h_hi, wdt_hi_ref[...]) + _dot(h_hi, wdt_lo_ref[...])
                       + _dot(h_lo, wdt_hi_ref[...]))

    acc = _dot(h_ref[...], w_ref[...])

    @pl.when(j < n_col - 1)
    def _():
        proj_ref[...] = acc

    @pl.when(j == n_col - 1)
    def _():
        v_ref[...] = acc


def _in_proj(x2d, norm_w, w_main, wdt_hi, wdt_lo, tm):
    n = x2d.shape[0]
    n_col = w_main.shape[1] // PROJ_TILE
    n_proj = PROJ_COLS // PROJ_TILE
    vmem = (2 * tm * D_MODEL * 4 + 2 * D_MODEL * PROJ_TILE * 2 + 4 * tm * PROJ_TILE * 4
            + 2 * tm * DT_PAD * 4 + tm * D_MODEL * 2 + 3 * tm * PROJ_TILE * 4 + (4 << 20))
    return pl.pallas_call(
        functools.partial(_in_proj_kernel, n_col=n_col),
        grid=(n // tm, n_col),
        in_specs=[
            pl.BlockSpec((tm, D_MODEL), lambda i, j: (i, 0)),
            pl.BlockSpec((1, D_MODEL), lambda i, j: (0, 0)),
            pl.BlockSpec((D_MODEL, PROJ_TILE), lambda i, j: (0, j)),
            pl.BlockSpec((D_MODEL, DT_PAD), lambda i, j: (0, 0)),
            pl.BlockSpec((D_MODEL, DT_PAD), lambda i, j: (0, 0)),
        ],
        out_specs=[
            pl.BlockSpec((tm, PROJ_TILE), lambda i, j: (i, jnp.minimum(j, n_proj - 1))),
            pl.BlockSpec((tm, PROJ_TILE), lambda i, j: (i, 0)),
            pl.BlockSpec((tm, DT_PAD), lambda i, j: (i, 0)),
        ],
        out_shape=[
            jax.ShapeDtypeStruct((n, PROJ_COLS), F32),
            jax.ShapeDtypeStruct((n, D_ATT), F32),
            jax.ShapeDtypeStruct((n, DT_PAD), F32),
        ],
        scratch_shapes=[pltpu.VMEM((tm, D_MODEL), BF16)],
        compiler_params=pltpu.CompilerParams(
            dimension_semantics=("arbitrary", "arbitrary"), vmem_limit_bytes=_vmem_limit(vmem)),
        name="in_proj",
    )(x2d, norm_w, w_main, wdt_hi, wdt_lo)


def _ssd_kernel(xbc_ref, z_ref, dt_ref, convp_ref, ssmp_ref, convw_ref, convb_ref, dtb_ref, alog_ref,
                dskip_ref, normw_ref, tri_ref, expand_ref, expand_t_ref,
                y_ref, ssm_out_ref, h_ref, ext_ref, ybuf_ref, *, L, T, n_chunks):
    c = pl.program_id(1)
    gn = SSM_GROUPS * SSM_STATE
    hpg = SSM_HEADS // SSM_GROUPS
    gw = hpg * SSM_HEAD_DIM

    @pl.when(c == 0)
    def _():
        h_ref[...] = ssmp_ref[0]
        ext_ref[0:SUBLANES, :] = convp_ref[0]

    ext_ref[SUBLANES:SUBLANES + T, :] = xbc_ref[0]
    if T < L:
        ext_ref[SUBLANES + T:SUBLANES + L, :] = jnp.zeros((L - T, CONV_DIM), F32)

    cw = convw_ref[...]
    conv = convb_ref[...] + cw[3:4, :] * ext_ref[SUBLANES:SUBLANES + L, :]
    conv = conv + cw[2:3, :] * ext_ref[SUBLANES - 1:SUBLANES - 1 + L, :]
    conv = conv + cw[1:2, :] * ext_ref[SUBLANES - 2:SUBLANES - 2 + L, :]
    conv = conv + cw[0:1, :] * ext_ref[SUBLANES - 3:SUBLANES - 3 + L, :]
    ext_ref[0:SUBLANES, :] = ext_ref[L:L + SUBLANES, :]

    xbc = conv * _sigmoid(conv)
    xs = xbc[:, :D_INNER]
    bm = xbc[:, D_INNER:D_INNER + gn]
    cm = xbc[:, D_INNER + gn:]

    row_l = lax.broadcasted_iota(jnp.int32, (L, DT_PAD), 0)
    if T < L:
        dt_raw = jnp.concatenate([dt_ref[0], jnp.zeros((L - T, DT_PAD), F32)], axis=0)
    else:
        dt_raw = dt_ref[0]
    dt = jnp.where(row_l < T, _softplus(dt_raw + dtb_ref[...]), 0.0)
    a = -jnp.exp(alog_ref[...])
    dta = dt * a
    d_hi, d_mid, d_lo = _split3(dta)
    tri = tri_ref[...]
    acum = _dot(tri, d_hi) + _dot(tri, d_mid) + _dot(tri, d_lo)
    acum_t = acum.T
    last = acum[L - 1:L, :]
    e_last = jnp.exp(last)
    stacked = jnp.concatenate([dt, jnp.exp(acum), jnp.exp(last - acum) * dt], axis=0)
    s_hi, s_lo = _split2(stacked)
    ex = expand_ref[...]
    wide = _dot(s_hi, ex) + _dot(s_lo, ex)
    dt_x = wide[0:L]
    ea_x = wide[L:2 * L]
    elm_x = wide[2 * L:3 * L]
    xdt_b = (xs * dt_x).astype(BF16)
    xdec_b = (xs * elm_x).astype(BF16)

    row = lax.broadcasted_iota(jnp.int32, (L, L), 0)
    col = lax.broadcasted_iota(jnp.int32, (L, L), 1)
    causal = row >= col
    lane = lax.broadcasted_iota(jnp.int32, (L, 2 * SSM_HEAD_DIM), 1)
    first_head = lane < SSM_HEAD_DIM

    for g in range(SSM_GROUPS):
        bg = bm[:, g * SSM_STATE:(g + 1) * SSM_STATE].astype(BF16)
        cg = cm[:, g * SSM_STATE:(g + 1) * SSM_STATE].astype(BF16)
        cb = _dot_nt(cg, bg)
        for pr in range(hpg // 2):
            ha = g * hpg + 2 * pr
            c0 = ha * SSM_HEAD_DIM
            xp = xdt_b[:, c0:c0 + 2 * SSM_HEAD_DIM]
            ys = []
            for hh in (ha, ha + 1):
                dec = jnp.exp(jnp.where(causal, acum[:, hh:hh + 1] - acum_t[hh:hh + 1, :], NEG_INF))
                ys.append(_dot((cb * dec).astype(BF16), xp))
            ybuf_ref[:, c0:c0 + 2 * SSM_HEAD_DIM] = jnp.where(first_head, ys[0], ys[1])
        r0 = g * gw
        hg = h_ref[r0:r0 + gw, :]
        y_state = _dot_nt(cg, hg.astype(BF16)) * ea_x[:, r0:r0 + gw]
        ybuf_ref[:, r0:r0 + gw] = ybuf_ref[:, r0:r0 + gw] + y_state
        upd = _dot_tn(xdec_b[:, r0:r0 + gw], bg)
        el_col = jnp.sum(expand_t_ref[r0:r0 + gw, :] * e_last, axis=1, keepdims=True)
        h_ref[r0:r0 + gw, :] = hg * el_col + upd

    y = ybuf_ref[...] + dskip_ref[...] * xs
    y = y[0:T]
    z = z_ref[0]
    y = y * (z * _sigmoid(z))
    for g in range(SSM_GROUPS):
        yg = y[:, g * gw:(g + 1) * gw]
        ms = jnp.mean(yg * yg, axis=-1, keepdims=True)
        y_ref[0, :, g * gw:(g + 1) * gw] = (yg * lax.rsqrt(ms + RMS_EPS) * normw_ref[:, g * gw:(g + 1) * gw]).astype(BF16)

    @pl.when(c == n_chunks - 1)
    def _():
        ssm_out_ref[0] = h_ref[...]


def _ssd(proj3, dt3, conv_prev8, ssm_prev, conv_w8, conv_b, dt_bias, a_log, d_skip_x, norm_w, L, T):
    b, t_total, _ = proj3.shape
    n_chunks = t_total // T
    tri = (jnp.arange(L)[:, None] >= jnp.arange(L)[None, :]).astype(BF16)
    head_of = jnp.arange(D_INNER) // SSM_HEAD_DIM
    expand = (jnp.arange(DT_PAD)[:, None] == head_of[None, :]).astype(BF16)
    expand_t = (head_of[:, None] == jnp.arange(DT_PAD)[None, :]).astype(F32)
    const = lambda bi, ci: (0, 0)
    vmem = (2 * T * CONV_DIM * 4 + 2 * T * D_INNER * 4 + 4 * D_INNER * SSM_STATE * 4 + (L + SUBLANES) * CONV_DIM * 4
            + L * D_INNER * 4 + 2 * T * D_INNER * 2 + 2 * (DT_PAD * D_INNER * 2 + D_INNER * DT_PAD * 4)
            + 14 * L * D_INNER * 4 + (8 << 20))
    return pl.pallas_call(
        functools.partial(_ssd_kernel, L=L, T=T, n_chunks=n_chunks),
        grid=(b, n_chunks),
        in_specs=[
            pl.BlockSpec((1, T, CONV_DIM), lambda bi, ci: (bi, ci, COL_XBC // CONV_DIM)),
            pl.BlockSpec((1, T, D_INNER), lambda bi, ci: (bi, ci, COL_Z // D_INNER)),
            pl.BlockSpec((1, T, DT_PAD), lambda bi, ci: (bi, ci, 0)),
            pl.BlockSpec((1, SUBLANES, CONV_DIM), lambda bi, ci: (bi, 0, 0)),
            pl.BlockSpec((1, D_INNER, SSM_STATE), lambda bi, ci: (bi, 0, 0)),
            pl.BlockSpec((SUBLANES, CONV_DIM), const),
            pl.BlockSpec((1, CONV_DIM), const),
            pl.BlockSpec((1, DT_PAD), const),
            pl.BlockSpec((1, DT_PAD), const),
            pl.BlockSpec((1, D_INNER), const),
            pl.BlockSpec((1, D_INNER), const),
            pl.BlockSpec((L, L), const),
            pl.BlockSpec((DT_PAD, D_INNER), const),
            pl.BlockSpec((D_INNER, DT_PAD), const),
        ],
        out_specs=[
            pl.BlockSpec((1, T, D_INNER), lambda bi, ci: (bi, ci, 0)),
            pl.BlockSpec((1, D_INNER, SSM_STATE), lambda bi, ci: (bi, 0, 0)),
        ],
        out_shape=[
            jax.ShapeDtypeStruct((b, t_total, D_INNER), BF16),
            jax.ShapeDtypeStruct((b, D_INNER, SSM_STATE), F32),
        ],
        scratch_shapes=[
            pltpu.VMEM((D_INNER, SSM_STATE), F32),
            pltpu.VMEM((L + SUBLANES, CONV_DIM), F32),
            pltpu.VMEM((L, D_INNER), F32),
        ],
        compiler_params=pltpu.CompilerParams(
            dimension_semantics=("arbitrary", "arbitrary"), vmem_limit_bytes=_vmem_limit(vmem)),
        name="ssd",
    )(proj3, proj3, dt3, conv_prev8, ssm_prev, conv_w8, conv_b, dt_bias, a_log, d_skip_x, norm_w,
      tri, expand, expand_t)


def _norm_rope_t(x_t, w_col, cos, sin):
    outs = []
    for h in range(2):
        xh = x_t[h * ATT_HEAD_DIM:(h + 1) * ATT_HEAD_DIM, :]
        ms = jnp.mean(xh * xh, axis=0, keepdims=True)
        xn = xh * lax.rsqrt(ms + RMS_EPS) * w_col[h * ATT_HEAD_DIM:(h + 1) * ATT_HEAD_DIM, :]
        x1 = xn[0:ROPE_HALF, :]
        x2 = xn[ROPE_HALF:ROPE_DIM, :]
        outs += [x1 * cos - x2 * sin, x2 * cos + x1 * sin, xn[ROPE_DIM:, :]]
    return jnp.concatenate(outs, axis=0)


def _moba_prompt_kernel(q_ref, k_ref, v_ref, qw_ref, kw_ref, cos_ref, sin_ref, y_ref, kout_ref,
                        qta_ref, qtb_ref, qtf_ref, kn_ref, vt_ref, km_ref, bias_ref, *, n_blocks):
    blk = MOBA_BLOCK
    pair_w = 2 * ATT_HEAD_DIM
    sub = lax.broadcasted_iota(jnp.int32, (pair_w, blk), 0)
    head_a = sub < ATT_HEAD_DIM
    scale = ATT_HEAD_DIM ** -0.5

    def prep(i, carry):
        r0 = pl.multiple_of(i * blk, blk)
        cos = cos_ref[i]
        sin = sin_ref[i]
        qn = _norm_rope_t(q_ref[0, pl.ds(r0, blk), :].T, qw_ref[...], cos, sin)
        qtf_ref[i] = qn
        qs = qn * scale
        qta_ref[i] = jnp.where(head_a, qs, 0.0).astype(BF16)
        qtb_ref[i] = jnp.where(head_a, 0.0, qs).astype(BF16)
        kn = _norm_rope_t(k_ref[0, pl.ds(r0, blk), :].T, kw_ref[...], cos, sin).T
        kout_ref[0, pl.ds(r0, blk), :] = kn
        kn_ref[i] = kn.astype(BF16)
        km_ref[pl.ds(i, 1), :] = jnp.mean(kn, axis=0, keepdims=True)
        vt_ref[i] = v_ref[0, pl.ds(r0, blk), :].T.astype(BF16)
        return carry

    lax.fori_loop(0, n_blocks, prep, 0)

    km = km_ref[...]
    km_lane = lax.broadcasted_iota(jnp.int32, km.shape, 1)
    jrow = lax.broadcasted_iota(jnp.int32, (n_blocks, blk), 0)
    key_i = lax.broadcasted_iota(jnp.int32, (blk, blk), 0)
    qry_i = lax.broadcasted_iota(jnp.int32, (blk, blk), 1)
    k_sel = min(MOBA_TOPK, n_blocks)

    def attend(i, carry):
        r0 = pl.multiple_of(i * blk, blk)
        qtf = qtf_ref[i]
        outs = []
        for h in range(2):
            in_head = (km_lane < ATT_HEAD_DIM) if h == 0 else (km_lane >= ATT_HEAD_DIM)
            gate = _dot_f32(jnp.where(in_head, km, 0.0), qtf)
            valid = jrow < i
            gate = jnp.where(valid, gate, NEG_INF)
            cnt = jnp.zeros((n_blocks, blk), jnp.int32)
            for jp in range(n_blocks):
                gj = gate[jp:jp + 1, :]
                better = (gj > gate) | ((gj == gate) & (jp < jrow))
                cnt = cnt + better.astype(jnp.int32)
            sel = valid & (cnt < k_sel)
            bias_ref[h] = jnp.where(sel, 0.0, NEG_INF)

            qt = (qta_ref if h == 0 else qtb_ref)[i]
            s = _dot(kn_ref[i], qt)
            s = jnp.where(key_i <= qry_i, s, NEG_INF)
            m = jnp.max(s, axis=0, keepdims=True)
            p = jnp.exp(s - m)
            l = jnp.sum(p, axis=0, keepdims=True)
            acc = _dot(vt_ref[i], p.astype(BF16))

            def body(j, st, h=h, qt=qt):
                m0, l0, a0 = st
                s1 = _dot(kn_ref[j], qt) + bias_ref[h, pl.ds(j, 1), :]
                m1 = jnp.maximum(m0, jnp.max(s1, axis=0, keepdims=True))
                alpha = jnp.exp(m0 - m1)
                p1 = jnp.exp(s1 - m1)
                return (m1, alpha * l0 + jnp.sum(p1, axis=0, keepdims=True),
                        alpha * a0 + _dot(vt_ref[j], p1.astype(BF16)))

            m, l, acc = lax.fori_loop(0, i, body, (m, l, acc))
            outs.append(acc / l)
        o_t = jnp.where(head_a, outs[0], outs[1])
        y_ref[0, pl.ds(r0, blk), :] = o_t.T.astype(BF16)
        return carry

    lax.fori_loop(0, n_blocks, attend, 0)


def _moba_prompt(proj3, v3, q_norm_w, k_norm_w, cos_t, sin_t):
    b, s, _ = proj3.shape
    n_blocks = s // MOBA_BLOCK
    n_pairs = ATT_HEADS // 2
    pw = 2 * ATT_HEAD_DIM
    qw = jnp.tile(q_norm_w, 2).reshape(pw, 1)
    kw = jnp.tile(k_norm_w, 2).reshape(pw, 1)
    const3 = lambda bi, p: (0, 0, 0)
    vmem = (2 * 3 * s * pw * 4 + 2 * s * pw * 2 + 2 * s * pw * 4 + 4 * s * pw * 2 + s * pw * 4 + (12 << 20))
    return pl.pallas_call(
        functools.partial(_moba_prompt_kernel, n_blocks=n_blocks),
        grid=(b, n_pairs),
        in_specs=[
            pl.BlockSpec((1, s, pw), lambda bi, p: (bi, 0, COL_Q // pw + p)),
            pl.BlockSpec((1, s, pw), lambda bi, p: (bi, 0, COL_K // pw + p)),
            pl.BlockSpec((1, s, pw), lambda bi, p: (bi, 0, p)),
            pl.BlockSpec((pw, 1), lambda bi, p: (0, 0)),
            pl.BlockSpec((pw, 1), lambda bi, p: (0, 0)),
            pl.BlockSpec((n_blocks, ROPE_HALF, MOBA_BLOCK), const3),
            pl.BlockSpec((n_blocks, ROPE_HALF, MOBA_BLOCK), const3),
        ],
        out_specs=[
            pl.BlockSpec((1, s, pw), lambda bi, p: (bi, 0, p)),
            pl.BlockSpec((1, s, pw), lambda bi, p: (bi, 0, p)),
        ],
        out_shape=[
            jax.ShapeDtypeStruct((b, s, D_ATT), BF16),
            jax.ShapeDtypeStruct((b, s, D_ATT), F32),
        ],
        scratch_shapes=[
            pltpu.VMEM((n_blocks, pw, MOBA_BLOCK), BF16),
            pltpu.VMEM((n_blocks, pw, MOBA_BLOCK), BF16),
            pltpu.VMEM((n_blocks, pw, MOBA_BLOCK), F32),
            pltpu.VMEM((n_blocks, MOBA_BLOCK, pw), BF16),
            pltpu.VMEM((n_blocks, pw, MOBA_BLOCK), BF16),
            pltpu.VMEM((n_blocks, pw), F32),
            pltpu.VMEM((2, n_blocks, MOBA_BLOCK), F32),
        ],
        compiler_params=pltpu.CompilerParams(
            dimension_semantics=("arbitrary", "arbitrary"), vmem_limit_bytes=_vmem_limit(vmem)),
        name="moba_prompt",
    )(proj3, proj3, v3, qw, kw, cos_t, sin_t)


def _seg_sum(x, seg):
    w = x.shape[-1]
    lane = lax.broadcasted_iota(jnp.int32, x.shape, x.ndim - 1)
    s = 1
    while s < seg:
        x = x + jnp.where((lane & s) != 0, pltpu.roll(x, s, x.ndim - 1), pltpu.roll(x, w - s, x.ndim - 1))
        s *= 2
    return x


def _norm_rope_rows(x, w_row, cosf, sin_lo, sin_hi):
    ms = _seg_sum(x * x, ATT_HEAD_DIM) * (1.0 / ATT_HEAD_DIM)
    xn = x * lax.rsqrt(ms + RMS_EPS) * w_row
    width = x.shape[-1]
    return (xn * cosf + pltpu.roll(xn, width - ROPE_HALF, 1) * sin_lo + pltpu.roll(xn, ROPE_HALF, 1) * sin_hi)


def _moba_sample_kernel(pt_ref, q_ref, k_ref, v_ref, qw_ref, kw_ref, cosf_ref, slo_ref, shi_ref, hbias_ref, hmask_ref,
                        *rest, T, pages_per_step, n_steps, n_blocks):
    kp = rest[:pages_per_step]
    vp = rest[pages_per_step:2 * pages_per_step]
    y_ref, kout_ref = rest[2 * pages_per_step:2 * pages_per_step + 2]
    qall_ref, qb_ref, kall_ref, vall_ref, ms_ref, ls_ref, gs_ref, o_ref = rest[2 * pages_per_step + 2:]
    step = pl.program_id(1)
    rows = ATT_HEADS * T
    hd = ATT_HEAD_DIM
    scale = ATT_HEAD_DIM ** -0.5

    @pl.when(step == 0)
    def _():
        qn = _norm_rope_rows(q_ref[0], qw_ref[...], cosf_ref[...], slo_ref[...], shi_ref[...])
        kn = _norm_rope_rows(k_ref[0], kw_ref[...], cosf_ref[...], slo_ref[...], shi_ref[...])
        kout_ref[0] = kn
        v = v_ref[0]
        for h in range(ATT_HEADS):
            qh = qn[:, h * hd:(h + 1) * hd]
            qall_ref[h * T:(h + 1) * T, :] = qh
            qb_ref[h * T:(h + 1) * T, :] = (qh * scale).astype(BF16)
            kall_ref[h * T:(h + 1) * T, :] = kn[:, h * hd:(h + 1) * hd].astype(BF16)
            vall_ref[h * T:(h + 1) * T, :] = v[:, h * hd:(h + 1) * hd].astype(BF16)
        ms_ref[...] = jnp.full((rows, LANES), NEG_INF, F32)
        gs_ref[...] = jnp.full((rows, LANES), NEG_INF, F32)
        ls_ref[...] = jnp.zeros((rows, LANES), F32)

    lane = lax.broadcasted_iota(jnp.int32, (rows, LANES), 1)
    qb = qb_ref[...]
    qall = qall_ref[...]
    blocks_per_step = pages_per_step // PAGES_PER_BLOCK
    for jj in range(blocks_per_step):
        j = step * blocks_per_step + jj
        kf = jnp.concatenate([kp[jj * PAGES_PER_BLOCK + u][0, 0].reshape(PAGE_SIZE * ATT_HEADS, hd)
                              for u in range(PAGES_PER_BLOCK)], axis=0)
        vf = jnp.concatenate([vp[jj * PAGES_PER_BLOCK + u][0, 0].reshape(PAGE_SIZE * ATT_HEADS, hd)
                              for u in range(PAGES_PER_BLOCK)], axis=0)
        k_hi, k_lo = _split2(kf)
        s_t = _dot_nt(qb, k_hi) + hbias_ref[...]
        m_j = jnp.max(s_t, axis=1, keepdims=True)
        p = jnp.exp(s_t - m_j)
        l_j = jnp.sum(p, axis=1, keepdims=True)
        o_ref[j] = _dot(p.astype(BF16), vf.astype(BF16))
        hm = hmask_ref[...]
        kme = (_dot(hm, k_hi) + _dot(hm, k_lo)) * (1.0 / MOBA_BLOCK)
        g_j = jnp.sum(qall * kme, axis=1, keepdims=True)
        here = lane == j
        ms_ref[...] = jnp.where(here, m_j, ms_ref[...])
        ls_ref[...] = jnp.where(here, l_j, ls_ref[...])
        gs_ref[...] = jnp.where(here, g_j, gs_ref[...])

    @pl.when(step == n_steps - 1)
    def _():
        r = lax.broadcasted_iota(jnp.int32, (rows, rows), 0)
        c = lax.broadcasted_iota(jnp.int32, (rows, rows), 1)
        own_ok = ((r // T) == (c // T)) & ((c % T) <= (r % T))
        s_own = jnp.where(own_ok, _dot_nt(qb, kall_ref[...]), NEG_INF)
        m_o = jnp.max(s_own, axis=1, keepdims=True)
        p_o = jnp.exp(s_own - m_o)
        l_o = jnp.sum(p_o, axis=1, keepdims=True)
        o_o = _dot(p_o.astype(BF16), vall_ref[...])

        gs = gs_ref[...]
        cnt = jnp.zeros((rows, LANES), jnp.int32)
        for jp in range(n_blocks):
            gj = gs[:, jp:jp + 1]
            better = (gj > gs) | ((gj == gs) & (jp < lane))
            cnt = cnt + better.astype(jnp.int32)
        sel = (lane < n_blocks) & (cnt < min(MOBA_TOPK, n_blocks))
        ms = ms_ref[...]
        m_tot = jnp.maximum(jnp.max(jnp.where(sel, ms, NEG_INF), axis=1, keepdims=True), m_o)
        w = jnp.where(sel, jnp.exp(ms - m_tot), 0.0)
        w_o = jnp.exp(m_o - m_tot)
        l_tot = jnp.sum(w * ls_ref[...], axis=1, keepdims=True) + w_o * l_o
        acc = w_o * o_o
        for jp in range(n_blocks):
            acc = acc + w[:, jp:jp + 1] * o_ref[jp]
        out = acc / l_tot
        y_ref[0] = jnp.concatenate([out[h * T:(h + 1) * T, :] for h in range(ATT_HEADS)], axis=1).astype(BF16)


def _moba_sample(q3, k3, v3, cache_k, cache_v, page_table, layer, q_norm_w, k_norm_w, cosf, sin_lo, sin_hi):
    nseq, T, _ = v3.shape
    n_pages = page_table.shape[1]
    assert T == SUBLANES and n_pages % PAGES_PER_BLOCK == 0
    n_blocks = n_pages // PAGES_PER_BLOCK
    assert n_blocks <= LANES
    pages_per_step = 4
    assert n_pages % pages_per_step == 0
    n_steps = n_pages // pages_per_step
    rows = ATT_HEADS * T
    cols = MOBA_BLOCK * ATT_HEADS
    same_head = (jnp.arange(rows)[:, None] // T) == (jnp.arange(cols)[None, :] % ATT_HEADS)
    hbias = jnp.where(same_head, 0.0, NEG_INF).astype(F32)
    hmask = same_head.astype(BF16)
    qw = jnp.tile(q_norm_w, ATT_HEADS).reshape(1, D_ATT)
    kw = jnp.tile(k_norm_w, ATT_HEADS).reshape(1, D_ATT)

    tok = lambda b, s, pt: (b, 0, 0)
    const = lambda b, s, pt: (0, 0)
    page_shape = (1, 1, PAGE_SIZE, ATT_HEADS, ATT_HEAD_DIM)

    def page_spec(u):
        return pl.BlockSpec(page_shape, lambda b, s, pt, u=u: (layer, pt[b, s * pages_per_step + u], 0, 0, 0))

    in_specs = [
        pl.BlockSpec((1, T, D_ATT), lambda b, s, pt: (b, 0, COL_Q // D_ATT)),
        pl.BlockSpec((1, T, D_ATT), lambda b, s, pt: (b, 0, COL_K // D_ATT)),
        pl.BlockSpec((1, T, D_ATT), tok),
        pl.BlockSpec((1, D_ATT), const),
        pl.BlockSpec((1, D_ATT), const),
        pl.BlockSpec((T, D_ATT), const),
        pl.BlockSpec((T, D_ATT), const),
        pl.BlockSpec((T, D_ATT), const),
        pl.BlockSpec((rows, cols), const),
        pl.BlockSpec((rows, cols), const),
    ] + [page_spec(u) for u in range(pages_per_step)] * 2
    page_vmem = PAGE_SIZE * ATT_HEADS * LANES * 4
    vmem = 2 * 2 * pages_per_step * page_vmem + 2 * rows * cols * 6 + n_blocks * rows * LANES * 4 + (20 << 20)
    return pl.pallas_call(
        functools.partial(_moba_sample_kernel, T=T, pages_per_step=pages_per_step, n_steps=n_steps,
                          n_blocks=n_blocks),
        grid_spec=pltpu.PrefetchScalarGridSpec(
            num_scalar_prefetch=1,
            grid=(nseq, n_steps),
            in_specs=in_specs,
            out_specs=[pl.BlockSpec((1, T, D_ATT), tok), pl.BlockSpec((1, T, D_ATT), tok)],
            scratch_shapes=[
                pltpu.VMEM((rows, ATT_HEAD_DIM), F32),
                pltpu.VMEM((rows, ATT_HEAD_DIM), BF16),
                pltpu.VMEM((rows, ATT_HEAD_DIM), BF16),
                pltpu.VMEM((rows, ATT_HEAD_DIM), BF16),
                pltpu.VMEM((rows, LANES), F32),
                pltpu.VMEM((rows, LANES), F32),
                pltpu.VMEM((rows, LANES), F32),
                pltpu.VMEM((n_blocks, rows, ATT_HEAD_DIM), F32),
            ],
        ),
        out_shape=[
            jax.ShapeDtypeStruct((nseq, T, D_ATT), BF16),
            jax.ShapeDtypeStruct((nseq, T, D_ATT), F32),
        ],
        compiler_params=pltpu.CompilerParams(
            dimension_semantics=("arbitrary", "arbitrary"), vmem_limit_bytes=_vmem_limit(vmem)),
        name="moba_sample",
    )(page_table, q3, k3, v3, qw, kw, cosf, sin_lo, sin_hi, hbias, hmask,
      *([cache_k] * pages_per_step), *([cache_v] * pages_per_step))


def _merge_kernel(x_ref, yssm_ref, yatt_ref, g_ref, wssm_ref, watt_ref, wout_ref, n2w_ref, x1_ref, h2_ref):
    u_ssm = _dot(yssm_ref[...], wssm_ref[...])
    u_att = _dot(yatt_ref[...], watt_ref[...])
    g = _sigmoid(g_ref[...])
    merged = (g[:, :D_MODEL] * u_ssm + g[:, D_MODEL:] * u_att).astype(BF16)
    x1 = x_ref[...] + _dot(merged, wout_ref[...])
    x1_ref[...] = x1
    ms = jnp.mean(x1 * x1, axis=-1, keepdims=True)
    h2_ref[...] = (x1 * lax.rsqrt(ms + RMS_EPS) * n2w_ref[...]).astype(BF16)


def _merge(x2d, y_ssm, y_att, proj, w_ssm, w_att, w_out, norm2_w, tm):
    n = x2d.shape[0]
    const = lambda i: (0, 0)
    vmem = (2 * tm * (D_MODEL * 4 + D_INNER * 2 + D_ATT * 2 + 2 * D_MODEL * 4 + D_MODEL * 4 + D_MODEL * 2)
            + 2 * 2 * (D_INNER + D_ATT + D_MODEL) * D_MODEL + 8 * tm * D_MODEL * 4 + (4 << 20))
    return pl.pallas_call(
        _merge_kernel,
        grid=(n // tm,),
        in_specs=[
            pl.BlockSpec((tm, D_MODEL), lambda i: (i, 0)),
            pl.BlockSpec((tm, D_INNER), lambda i: (i, 0)),
            pl.BlockSpec((tm, D_ATT), lambda i: (i, 0)),
            pl.BlockSpec((tm, 2 * D_MODEL), lambda i: (i, COL_G // (2 * D_MODEL))),
            pl.BlockSpec((D_INNER, D_MODEL), const),
            pl.BlockSpec((D_ATT, D_MODEL), const),
            pl.BlockSpec((D_MODEL, D_MODEL), const),
            pl.BlockSpec((1, D_MODEL), const),
        ],
        out_specs=[pl.BlockSpec((tm, D_MODEL), lambda i: (i, 0)), pl.BlockSpec((tm, D_MODEL), lambda i: (i, 0))],
        out_shape=[jax.ShapeDtypeStruct((n, D_MODEL), F32), jax.ShapeDtypeStruct((n, D_MODEL), BF16)],
        compiler_params=pltpu.CompilerParams(
            dimension_semantics=("arbitrary",), vmem_limit_bytes=_vmem_limit(vmem)),
        name="merge",
    )(x2d, y_ssm, y_att, proj, w_ssm, w_att, w_out, norm2_w)


def _ffn_kernel(h2_ref, x1_ref, wg_ref, wu_ref, wd_ref, out_ref):
    f = pl.program_id(1)
    h2 = h2_ref[...]
    gate = _dot(h2, wg_ref[...])
    up = _dot(h2, wu_ref[...])
    act = (gate * _sigmoid(gate) * up).astype(BF16)
    part = _dot(act, wd_ref[...])

    @pl.when(f == 0)
    def _():
        out_ref[...] = x1_ref[...] + part

    @pl.when(f != 0)
    def _():
        out_ref[...] = out_ref[...] + part


def _ffn(h2, x1, w_gate, w_up, w_down, tm, tf):
    n = h2.shape[0]
    vmem = (2 * tm * D_MODEL * (2 + 4 + 4) + 2 * 3 * D_MODEL * tf * 2 + 6 * tm * tf * 4 + (4 << 20))
    return pl.pallas_call(
        _ffn_kernel,
        grid=(n // tm, D_FF // tf),
        in_specs=[
            pl.BlockSpec((tm, D_MODEL), lambda i, f: (i, 0)),
            pl.BlockSpec((tm, D_MODEL), lambda i, f: (i, 0)),
            pl.BlockSpec((D_MODEL, tf), lambda i, f: (0, f)),
            pl.BlockSpec((D_MODEL, tf), lambda i, f: (0, f)),
            pl.BlockSpec((tf, D_MODEL), lambda i, f: (f, 0)),
        ],
        out_specs=pl.BlockSpec((tm, D_MODEL), lambda i, f: (i, 0)),
        out_shape=jax.ShapeDtypeStruct((n, D_MODEL), F32),
        compiler_params=pltpu.CompilerParams(
            dimension_semantics=("arbitrary", "arbitrary"), vmem_limit_bytes=_vmem_limit(vmem)),
        name="ffn",
    )(h2, x1, w_gate, w_up, w_down)


def _rope_angles(pos):
    inv_freq = ROPE_THETA ** (-(jnp.arange(ROPE_HALF, dtype=F32) * 2.0 / ROPE_DIM))
    ang = pos.astype(F32)[:, None] * inv_freq[None, :]
    return jnp.cos(ang), jnp.sin(ang)


def _rope_tables_t(pos, n_blocks):
    cos, sin = _rope_angles(pos)
    to_blocks = lambda t: t.T.reshape(ROPE_HALF, n_blocks, MOBA_BLOCK).transpose(1, 0, 2)
    return to_blocks(cos), to_blocks(sin)


def _rope_tables_rows(pos):
    cos, sin = _rope_angles(pos)
    d = jnp.arange(D_ATT) % ATT_HEAD_DIM
    idx = d % ROPE_HALF
    cosf = jnp.where(d[None, :] < ROPE_DIM, cos[:, idx], 1.0)
    sin_lo = jnp.where(d[None, :] < ROPE_HALF, -sin[:, idx], 0.0)
    sin_hi = jnp.where((d[None, :] >= ROPE_HALF) & (d[None, :] < ROPE_DIM), sin[:, idx], 0.0)
    return cosf.astype(F32), sin_lo.astype(F32), sin_hi.astype(F32)


def _layer_weights(l, norm1_w, w_in, conv_w, conv_b, dt_bias, a_log, d_skip, ssm_norm_w, q_norm_w, k_norm_w,
                   w_ssm_branch, w_att_branch, w_out, norm2_w, w_ffn_in, w_ffn_out):
    w = w_in[l]
    off_xbc = D_INNER
    off_dt = off_xbc + CONV_DIM
    off_q = off_dt + SSM_HEADS
    off_k = off_q + D_ATT
    off_v = off_k + D_ATT
    off_g = off_v + D_ATT
    w_main = jnp.concatenate([w[:, off_xbc:off_dt], w[:, off_q:off_k], w[:, 0:off_xbc], w[:, off_g:],
                              w[:, off_k:off_v], w[:, off_v:off_g]], axis=1).astype(BF16)
    w_dt = jnp.pad(w[:, off_dt:off_q], ((0, 0), (0, DT_PAD - SSM_HEADS)))
    wdt_hi = w_dt.astype(BF16)
    wdt_lo = (w_dt - wdt_hi.astype(F32)).astype(BF16)
    pad_h = (0, DT_PAD - SSM_HEADS)
    return dict(
        norm1_w=norm1_w[l].reshape(1, D_MODEL), w_main=w_main, wdt_hi=wdt_hi, wdt_lo=wdt_lo,
        conv_w8=jnp.pad(conv_w[l], ((0, SUBLANES - CONV_WIDTH), (0, 0))), conv_b=conv_b[l].reshape(1, CONV_DIM),
        dt_bias=jnp.pad(dt_bias[l], pad_h).reshape(1, DT_PAD), a_log=jnp.pad(a_log[l], pad_h).reshape(1, DT_PAD),
        d_skip_x=jnp.repeat(d_skip[l], SSM_HEAD_DIM).reshape(1, D_INNER),
        ssm_norm_w=ssm_norm_w[l].reshape(1, D_INNER), q_norm_w=q_norm_w[l], k_norm_w=k_norm_w[l],
        w_ssm=w_ssm_branch[l].astype(BF16), w_att=w_att_branch[l].astype(BF16), w_out=w_out[l].astype(BF16),
        norm2_w=norm2_w[l].reshape(1, D_MODEL),
        w_gate=w_ffn_in[l][:, :D_FF].astype(BF16), w_up=w_ffn_in[l][:, D_FF:].astype(BF16),
        w_down=w_ffn_out[l].astype(BF16),
    )


def _trunk(x, conv_prev, ssm_prev, wts, attend, ssd_chunk, tm, tf):
    b, t, _ = x.shape
    n = b * t
    x2d = x.reshape(n, D_MODEL)
    proj, v, dt = _in_proj(x2d, wts["norm1_w"], wts["w_main"], wts["wdt_hi"], wts["wdt_lo"], tm)
    proj3 = proj.reshape(b, t, PROJ_COLS)
    v3 = v.reshape(b, t, D_ATT)
    conv_prev8 = jnp.pad(conv_prev, ((0, 0), (SUBLANES - (CONV_WIDTH - 1), 0), (0, 0)))
    L, T = ssd_chunk
    y_ssm, ssm_new = _ssd(proj3, dt.reshape(b, t, DT_PAD), conv_prev8, ssm_prev.reshape(b, D_INNER, SSM_STATE),
                          wts["conv_w8"], wts["conv_b"], wts["dt_bias"], wts["a_log"], wts["d_skip_x"],
                          wts["ssm_norm_w"], L, T)
    y_att, k3 = attend(proj3, v3)
    x1, h2 = _merge(x2d, y_ssm.reshape(n, D_INNER), y_att.reshape(n, D_ATT), proj, wts["w_ssm"], wts["w_att"],
                    wts["w_out"], wts["norm2_w"], tm)
    out = _ffn(h2, x1, wts["w_gate"], wts["w_up"], wts["w_down"], tm, tf)
    conv_new = proj3[:, t - (CONV_WIDTH - 1):, COL_XBC:COL_XBC + CONV_DIM]
    heads = (b, t, ATT_HEADS, ATT_HEAD_DIM)
    return (out.reshape(b, t, D_MODEL), k3.reshape(heads), v3.reshape(heads), conv_new,
            ssm_new.reshape(b, SSM_HEADS, SSM_HEAD_DIM, SSM_STATE))


def kernel(x_prompt, x_sample, cache_k, cache_v, page_table, state_conv, state_ssm, norm1_w, w_in, conv_w, conv_b, dt_bias, a_log, d_skip, ssm_norm_w, q_norm_w, k_norm_w, w_ssm_branch, w_att_branch, w_out, norm2_w, w_ffn_in, w_ffn_out):
    depth = w_in.shape[0]
    bp, sp, _ = x_prompt.shape
    bs, ts, _ = x_sample.shape
    past_len = page_table.shape[1] * PAGE_SIZE
    assert sp % MOBA_BLOCK == 0 and sp % SSD_CHUNK == 0 and ts <= SSD_CHUNK
    cos_t, sin_t = _rope_tables_t(jnp.arange(sp), sp // MOBA_BLOCK)
    cosf, sin_lo, sin_hi = _rope_tables_rows(past_len + jnp.arange(ts))
    xp, xs = x_prompt, x_sample
    outs = [[] for _ in range(8)]
    for l in range(depth):
        wts = _layer_weights(l, norm1_w, w_in, conv_w, conv_b, dt_bias, a_log, d_skip, ssm_norm_w, q_norm_w,
                             k_norm_w, w_ssm_branch, w_att_branch, w_out, norm2_w, w_ffn_in, w_ffn_out)
        attend_p = lambda proj3, v3: _moba_prompt(proj3, v3, wts["q_norm_w"], wts["k_norm_w"], cos_t, sin_t)
        conv0 = jnp.zeros((bp, CONV_WIDTH - 1, CONV_DIM), F32)
        ssm0 = jnp.zeros((bp, SSM_HEADS, SSM_HEAD_DIM, SSM_STATE), F32)
        xp, kp, vp, cp, ssp = _trunk(xp, conv0, ssm0, wts, attend_p, (SSD_CHUNK, SSD_CHUNK), 512, D_FF // 2)
        attend_s = lambda proj3, v3: _moba_sample(proj3, proj3, v3, cache_k, cache_v, page_table, l,
                                                  wts["q_norm_w"], wts["k_norm_w"], cosf, sin_lo, sin_hi)
        xs, ksn, vsn, cs, sss = _trunk(xs, state_conv[l], state_ssm[l], wts, attend_s, (LANES, ts), bs * ts,
                                       D_FF // 2)
        for lst, val in zip(outs, (kp, vp, ksn, vsn, cp, cs, ssp, sss)):
            lst.append(val)
    return (xp, xs) + tuple(jnp.stack(o) for o in outs)
```

```python
import functools

import jax
import jax.numpy as jnp
from jax import lax
from jax.experimental import pallas as pl
from jax.experimental.pallas import tpu as pltpu

F32 = jnp.float32
BF16 = jnp.bfloat16

D_MODEL = 1024
D_INNER = 2048
SSM_HEADS = 32
SSM_HEAD_DIM = 64
SSM_GROUPS = 4
SSM_STATE = 128
CONV_WIDTH = 4
CONV_DIM = D_INNER + 2 * SSM_GROUPS * SSM_STATE
SSD_CHUNK = 256
ATT_HEADS = 16
ATT_HEAD_DIM = 64
D_ATT = ATT_HEADS * ATT_HEAD_DIM
ROPE_DIM = 16
ROPE_HALF = ROPE_DIM // 2
ROPE_THETA = 500000.0
MOBA_BLOCK = 256
MOBA_TOPK = 3
PAGE_SIZE = 128
PAGES_PER_BLOCK = MOBA_BLOCK // PAGE_SIZE
D_FF = 2816
RMS_EPS = 1e-6
NEG_INF = float("-inf")
MASK_BIAS = -1e30
LOG2_E = 1.4426950408889634

LANES = 128
SUBLANES = 8
V7X_VMEM_BYTES = 64 * 1024 * 1024

COL_XBC = 0
COL_Q = COL_XBC + CONV_DIM
COL_Z = COL_Q + D_ATT
COL_G = COL_Z + D_INNER
COL_K = COL_G + 2 * D_MODEL
PROJ_COLS = COL_K + D_ATT
PROJ_TILE = 1024
DT_PAD = LANES


def _vmem_limit(nbytes):
    return int(min(nbytes, V7X_VMEM_BYTES - 4 * 1024 * 1024))


def _dot(a, b):
    return jnp.dot(a, b, preferred_element_type=F32)


def _dot_nt(a, b):
    return lax.dot_general(a, b, (((1,), (1,)), ((), ())), preferred_element_type=F32)


def _dot_tn(a, b):
    return lax.dot_general(a, b, (((0,), (0,)), ((), ())), preferred_element_type=F32)


def _split2(x):
    hi = x.astype(BF16)
    lo = (x - hi.astype(F32)).astype(BF16)
    return hi, lo


def _split3(x):
    hi = x.astype(BF16)
    r = x - hi.astype(F32)
    mid = r.astype(BF16)
    lo = (r - mid.astype(F32)).astype(BF16)
    return hi, mid, lo


def _dot_f32(a, b):
    a_hi, a_lo = _split2(a)
    b_hi, b_lo = _split2(b)
    return _dot(a_hi, b_hi) + _dot(a_hi, b_lo) + _dot(a_lo, b_hi)


def _sigmoid(x):
    return 1.0 / (1.0 + jnp.exp(-x))


def _softplus(x):
    return jnp.maximum(x, 0.0) + jnp.log1p(jnp.exp(-jnp.abs(x)))


def _in_proj_kernel(x_ref, nw_ref, w_ref, wdt_hi_ref, wdt_lo_ref, proj_ref, v_ref, dt_ref, h_ref, *, n_col):
    j = pl.program_id(1)

    @pl.when(j == 0)
    def _():
        x = x_ref[...]
        ms = jnp.mean(x * x, axis=-1, keepdims=True)
        h = x * lax.rsqrt(ms + RMS_EPS) * nw_ref[...]
        h_hi, h_lo = _split2(h)
        h_ref[...] = h_hi
        dt_ref[...] = (_dot(h_hi, wdt_hi_ref[...]) + _dot(h_hi, wdt_lo_ref[...])
                       + _dot(h_lo, wdt_hi_ref[...]))

    acc = _dot(h_ref[...], w_ref[...])

    @pl.when(j < n_col - 1)
    def _():
        proj_ref[...] = acc

    @pl.when(j == n_col - 1)
    def _():
        v_ref[...] = acc


def _in_proj(x2d, norm_w, w_main, wdt_hi, wdt_lo, tm):
    n = x2d.shape[0]
    n_col = w_main.shape[1] // PROJ_TILE
    n_proj = PROJ_COLS // PROJ_TILE
    vmem = (2 * tm * D_MODEL * 4 + 2 * D_MODEL * PROJ_TILE * 2 + 4 * tm * PROJ_TILE * 4
            + 2 * tm * DT_PAD * 4 + tm * D_MODEL * 2 + 3 * tm * PROJ_TILE * 4 + (4 << 20))
    return pl.pallas_call(
        functools.partial(_in_proj_kernel, n_col=n_col),
        grid=(n // tm, n_col),
        in_specs=[
            pl.BlockSpec((tm, D_MODEL), lambda i, j: (i, 0)),
            pl.BlockSpec((1, D_MODEL), lambda i, j: (0, 0)),
            pl.BlockSpec((D_MODEL, PROJ_TILE), lambda i, j: (0, j)),
            pl.BlockSpec((D_MODEL, DT_PAD), lambda i, j: (0, 0)),
            pl.BlockSpec((D_MODEL, DT_PAD), lambda i, j: (0, 0)),
        ],
        out_specs=[
            pl.BlockSpec((tm, PROJ_TILE), lambda i, j: (i, jnp.minimum(j, n_proj - 1))),
            pl.BlockSpec((tm, PROJ_TILE), lambda i, j: (i, 0)),
            pl.BlockSpec((tm, DT_PAD), lambda i, j: (i, 0)),
        ],
        out_shape=[
            jax.ShapeDtypeStruct((n, PROJ_COLS), F32),
            jax.ShapeDtypeStruct((n, D_ATT), F32),
            jax.ShapeDtypeStruct((n, DT_PAD), F32),
        ],
        scratch_shapes=[pltpu.VMEM((tm, D_MODEL), BF16)],
        compiler_params=pltpu.CompilerParams(
            dimension_semantics=("arbitrary", "arbitrary"), vmem_limit_bytes=_vmem_limit(vmem)),
        name="in_proj",
    )(x2d, norm_w, w_main, wdt_hi, wdt_lo)


def _ssd_kernel(xbc_ref, z_ref, dt_ref, convp_ref, ssmp_ref, convw_ref, convb_ref, dtb_ref, alog_ref,
                dskip_ref, normw_ref, tri_ref, expand_ref, expand_t_ref,
                y_ref, ssm_out_ref, h_ref, ext_ref, ybuf_ref, *, L, T, n_chunks):
    c = pl.program_id(1)
    gn = SSM_GROUPS * SSM_STATE
    hpg = SSM_HEADS // SSM_GROUPS
    gw = hpg * SSM_HEAD_DIM

    @pl.when(c == 0)
    def _():
        h_ref[...] = ssmp_ref[0]
        ext_ref[0:SUBLANES, :] = convp_ref[0]

    ext_ref[SUBLANES:SUBLANES + T, :] = xbc_ref[0]
    if T < L:
        ext_ref[SUBLANES + T:SUBLANES + L, :] = jnp.zeros((L - T, CONV_DIM), F32)

    cw = convw_ref[...]
    conv = convb_ref[...] + cw[3:4, :] * ext_ref[SUBLANES:SUBLANES + L, :]
    conv = conv + cw[2:3, :] * ext_ref[SUBLANES - 1:SUBLANES - 1 + L, :]
    conv = conv + cw[1:2, :] * ext_ref[SUBLANES - 2:SUBLANES - 2 + L, :]
    conv = conv + cw[0:1, :] * ext_ref[SUBLANES - 3:SUBLANES - 3 + L, :]
    ext_ref[0:SUBLANES, :] = ext_ref[L:L + SUBLANES, :]

    xbc = conv * _sigmoid(conv)
    xs = xbc[:, :D_INNER]
    bm = xbc[:, D_INNER:D_INNER + gn]
    cm = xbc[:, D_INNER + gn:]

    row_l = lax.broadcasted_iota(jnp.int32, (L, DT_PAD), 0)
    if T < L:
        dt_raw = jnp.concatenate([dt_ref[0], jnp.zeros((L - T, DT_PAD), F32)], axis=0)
    else:
        dt_raw = dt_ref[0]
    dt = jnp.where(row_l < T, _softplus(dt_raw + dtb_ref[...]), 0.0)
    a = -jnp.exp(alog_ref[...])
    dta = dt * a
    d_hi, d_mid, d_lo = _split3(dta)
    tri = tri_ref[...]
    acum = _dot(tri, d_hi) + _dot(tri, d_mid) + _dot(tri, d_lo)
    acum_t = acum.T
    last = acum[L - 1:L, :]
    e_last = jnp.exp(last)
    stacked = jnp.concatenate([dt, jnp.exp(acum), jnp.exp(last - acum) * dt], axis=0)
    s_hi, s_lo = _split2(stacked)
    ex = expand_ref[...]
    wide = _dot(s_hi, ex) + _dot(s_lo, ex)
    dt_x = wide[0:L]
    ea_x = wide[L:2 * L]
    elm_x = wide[2 * L:3 * L]
    xdt_b = (xs * dt_x).astype(BF16)
    xdec_b = (xs * elm_x).astype(BF16)

    row = lax.broadcasted_iota(jnp.int32, (L, L), 0)
    col = lax.broadcasted_iota(jnp.int32, (L, L), 1)
    causal = row >= col
    lane = lax.broadcasted_iota(jnp.int32, (L, 2 * SSM_HEAD_DIM), 1)
    first_head = lane < SSM_HEAD_DIM

    for g in range(SSM_GROUPS):
        bg = bm[:, g * SSM_STATE:(g + 1) * SSM_STATE].astype(BF16)
        cg = cm[:, g * SSM_STATE:(g + 1) * SSM_STATE].astype(BF16)
        cb = _dot_nt(cg, bg)
        for pr in range(hpg // 2):
            ha = g * hpg + 2 * pr
            c0 = ha * SSM_HEAD_DIM
            xp = xdt_b[:, c0:c0 + 2 * SSM_HEAD_DIM]
            ys = []
            for hh in (ha, ha + 1):
                dec = jnp.exp(jnp.where(causal, acum[:, hh:hh + 1] - acum_t[hh:hh + 1, :], NEG_INF))
                ys.append(_dot((cb * dec).astype(BF16), xp))
            ybuf_ref[:, c0:c0 + 2 * SSM_HEAD_DIM] = jnp.where(first_head, ys[0], ys[1])
        r0 = g * gw
        hg = h_ref[r0:r0 + gw, :]
        y_state = _dot_nt(cg, hg.astype(BF16)) * ea_x[:, r0:r0 + gw]
        ybuf_ref[:, r0:r0 + gw] = ybuf_ref[:, r0:r0 + gw] + y_state
        upd = _dot_tn(xdec_b[:, r0:r0 + gw], bg)
        el_col = jnp.sum(expand_t_ref[r0:r0 + gw, :] * e_last, axis=1, keepdims=True)
        h_ref[r0:r0 + gw, :] = hg * el_col + upd

    y = ybuf_ref[...] + dskip_ref[...] * xs
    y = y[0:T]
    z = z_ref[0]
    y = y * (z * _sigmoid(z))
    for g in range(SSM_GROUPS):
        yg = y[:, g * gw:(g + 1) * gw]
        ms = jnp.mean(yg * yg, axis=-1, keepdims=True)
        y_ref[0, :, g * gw:(g + 1) * gw] = (yg * lax.rsqrt(ms + RMS_EPS) * normw_ref[:, g * gw:(g + 1) * gw]).astype(BF16)

    @pl.when(c == n_chunks - 1)
    def _():
        ssm_out_ref[0] = h_ref[...]


def _ssd(proj3, dt3, conv_prev8, ssm_prev, conv_w8, conv_b, dt_bias, a_log, d_skip_x, norm_w, L, T):
    b, t_total, _ = proj3.shape
    n_chunks = t_total // T
    tri = (jnp.arange(L)[:, None] >= jnp.arange(L)[None, :]).astype(BF16)
    head_of = jnp.arange(D_INNER) // SSM_HEAD_DIM
    expand = (jnp.arange(DT_PAD)[:, None] == head_of[None, :]).astype(BF16)
    expand_t = (head_of[:, None] == jnp.arange(DT_PAD)[None, :]).astype(F32)
    const = lambda bi, ci: (0, 0)
    vmem = (2 * T * CONV_DIM * 4 + 2 * T * D_INNER * 4 + 4 * D_INNER * SSM_STATE * 4 + (L + SUBLANES) * CONV_DIM * 4
            + L * D_INNER * 4 + 2 * T * D_INNER * 2 + 2 * (DT_PAD * D_INNER * 2 + D_INNER * DT_PAD * 4)
            + 14 * L * D_INNER * 4 + (8 << 20))
    return pl.pallas_call(
        functools.partial(_ssd_kernel, L=L, T=T, n_chunks=n_chunks),
        grid=(b, n_chunks),
        in_specs=[
            pl.BlockSpec((1, T, CONV_DIM), lambda bi, ci: (bi, ci, COL_XBC // CONV_DIM)),
            pl.BlockSpec((1, T, D_INNER), lambda bi, ci: (bi, ci, COL_Z // D_INNER)),
            pl.BlockSpec((1, T, DT_PAD), lambda bi, ci: (bi, ci, 0)),
            pl.BlockSpec((1, SUBLANES, CONV_DIM), lambda bi, ci: (bi, 0, 0)),
            pl.BlockSpec((1, D_INNER, SSM_STATE), lambda bi, ci: (bi, 0, 0)),
            pl.BlockSpec((SUBLANES, CONV_DIM), const),
            pl.BlockSpec((1, CONV_DIM), const),
            pl.BlockSpec((1, DT_PAD), const),
            pl.BlockSpec((1, DT_PAD), const),
            pl.BlockSpec((1, D_INNER), const),
            pl.BlockSpec((1, D_INNER), const),
            pl.BlockSpec((L, L), const),
            pl.BlockSpec((DT_PAD, D_INNER), const),
            pl.BlockSpec((D_INNER, DT_PAD), const),
        ],
        out_specs=[
            pl.BlockSpec((1, T, D_INNER), lambda bi, ci: (bi, ci, 0)),
            pl.BlockSpec((1, D_INNER, SSM_STATE), lambda bi, ci: (bi, 0, 0)),
        ],
        out_shape=[
            jax.ShapeDtypeStruct((b, t_total, D_INNER), BF16),
            jax.ShapeDtypeStruct((b, D_INNER, SSM_STATE), F32),
        ],
        scratch_shapes=[
            pltpu.VMEM((D_INNER, SSM_STATE), F32),
            pltpu.VMEM((L + SUBLANES, CONV_DIM), F32),
            pltpu.VMEM((L, D_INNER), F32),
        ],
        compiler_params=pltpu.CompilerParams(
            dimension_semantics=("arbitrary", "arbitrary"), vmem_limit_bytes=_vmem_limit(vmem)),
        name="ssd",
    )(proj3, proj3, dt3, conv_prev8, ssm_prev, conv_w8, conv_b, dt_bias, a_log, d_skip_x, norm_w,
      tri, expand, expand_t)


FLASH_CHUNK = 4


def _norm_rope_t(x_t, w_col, cos, sin):
    outs = []
    for h in range(2):
        xh = x_t[h * ATT_HEAD_DIM:(h + 1) * ATT_HEAD_DIM, :]
        ms = jnp.mean(xh * xh, axis=0, keepdims=True)
        xn = xh * lax.rsqrt(ms + RMS_EPS) * w_col[h * ATT_HEAD_DIM:(h + 1) * ATT_HEAD_DIM, :]
        x1 = xn[0:ROPE_HALF, :]
        x2 = xn[ROPE_HALF:ROPE_DIM, :]
        outs += [x1 * cos - x2 * sin, x2 * cos + x1 * sin, xn[ROPE_DIM:, :]]
    return jnp.concatenate(outs, axis=0)


def _moba_prompt_kernel(q_ref, k_ref, v_ref, qw_ref, kw_ref, cos_ref, sin_ref, y_ref, kout_ref,
                        qd_ref, qtf_ref, kn_ref, vtc_ref, vtb_ref, km_ref, s_ref, m_ref, l_ref, acc_ref,
                        *, n_blocks):
    blk = MOBA_BLOCK
    cw = FLASH_CHUNK
    n_chunks = n_blocks // cw
    hd = ATT_HEAD_DIM
    pair_w = 2 * hd
    scale = (ATT_HEAD_DIM ** -0.5) * LOG2_E
    lane_k = lax.broadcasted_iota(jnp.int32, (blk, pair_w), 1)

    for i in range(n_blocks):
        r0 = i * blk
        c0 = (i % cw) * blk
        cos = cos_ref[i]
        sin = sin_ref[i]
        qn = _norm_rope_t(q_ref[0, r0:r0 + blk, :].T, qw_ref[...], cos, sin)
        qtf_ref[i] = qn
        qs = (qn * scale).astype(BF16)
        qd_ref[0, i] = qs[0:hd, :]
        qd_ref[1, i] = qs[hd:pair_w, :]
        kn_t = _norm_rope_t(k_ref[0, r0:r0 + blk, :].T, kw_ref[...], cos, sin)
        kout_ref[0, :, r0:r0 + blk] = kn_t
        kn = kn_t.T
        km_ref[i:i + 1, :] = jnp.mean(kn, axis=0, keepdims=True)
        onehot = (lane_k == hd + i).astype(F32)
        kn_ref[0, i // cw, c0:c0 + blk, :] = jnp.where(lane_k < hd, kn, onehot).astype(BF16)
        kn_ref[1, i // cw, c0:c0 + blk, :] = jnp.where(lane_k < hd, pltpu.roll(kn, hd, 1), onehot).astype(BF16)
        vt = v_ref[0, r0:r0 + blk, :].T.astype(BF16)
        vtb_ref[i] = vt
        vtc_ref[i // cw, :, c0:c0 + blk] = vt

    km = km_ref[...]
    km_lane = lax.broadcasted_iota(jnp.int32, km.shape, 1)
    km_heads = (jnp.where(km_lane < hd, km, 0.0), jnp.where(km_lane < hd, 0.0, km))
    jrow = lax.broadcasted_iota(jnp.int32, (n_blocks, blk), 0)
    key_i = lax.broadcasted_iota(jnp.int32, (blk, blk), 0)
    qry_i = lax.broadcasted_iota(jnp.int32, (blk, blk), 1)
    k_sel = min(MOBA_TOPK, n_blocks)
    pad_rows = jnp.zeros((hd - n_blocks, blk), BF16)
    no_bias = jnp.zeros((hd, blk), BF16)

    def attend(i, carry):
        r0 = pl.multiple_of(i * blk, blk)
        ic = lax.div(i, cw)
        c0 = pl.multiple_of(lax.rem(i, cw) * blk, blk)
        qtf = qtf_ref[i]
        vd = vtb_ref[i]
        qts = []
        for h in range(2):
            gate = _dot_f32(km_heads[h], qtf)
            valid = jrow < i
            gate = jnp.where(valid, gate, NEG_INF)
            cnt = jnp.zeros((n_blocks, blk), jnp.int32)
            for jp in range(n_blocks):
                gj = gate[jp:jp + 1, :]
                better = (gj > gate) | ((gj == gate) & (jp < jrow))
                cnt = cnt + better.astype(jnp.int32)
            sel = valid & (cnt < k_sel)
            bias = jnp.where(sel, 0.0, MASK_BIAS).astype(BF16)
            qd = qd_ref[h, i]
            qts.append(jnp.concatenate([qd, bias, pad_rows], axis=0))

            s = _dot(kn_ref[h, ic, pl.ds(c0, blk), :], jnp.concatenate([qd, no_bias], axis=0))
            s = jnp.where(key_i <= qry_i, s, NEG_INF)
            m = jnp.max(s, axis=0, keepdims=True)
            p = jnp.exp2(s - m)
            m_ref[h] = m
            l_ref[h] = jnp.sum(p, axis=0, keepdims=True)
            acc_ref[h] = _dot(vd[h * hd:(h + 1) * hd, :], p.astype(BF16))
            s_ref[0, h] = _dot(kn_ref[h, 0], qts[h])

        for c in range(n_chunks):
            @pl.when(c * cw < i)
            def _(c=c):
                slot = c % 2
                for h in range(2):
                    if c + 1 < n_chunks:
                        s_ref[1 - slot, h] = _dot(kn_ref[h, c + 1], qts[h])
                    s = s_ref[slot, h]
                    m0 = m_ref[h]
                    m1 = jnp.maximum(m0, jnp.max(s, axis=0, keepdims=True))
                    alpha = jnp.exp2(m0 - m1)
                    p = jnp.exp2(s - m1)
                    m_ref[h] = m1
                    l_ref[h] = alpha * l_ref[h] + jnp.sum(p, axis=0, keepdims=True)
                    acc_ref[h] = alpha * acc_ref[h] + _dot(vtc_ref[c, h * hd:(h + 1) * hd, :], p.astype(BF16))

        o_t = jnp.concatenate([acc_ref[0] / l_ref[0], acc_ref[1] / l_ref[1]], axis=0)
        y_ref[0, pl.ds(r0, blk), :] = o_t.T.astype(BF16)
        return carry

    lax.fori_loop(0, n_blocks, attend, 0)


def _moba_prompt(proj3, v3, q_norm_w, k_norm_w, cos_t, sin_t):
    b, s, _ = proj3.shape
    n_blocks = s // MOBA_BLOCK
    assert n_blocks % FLASH_CHUNK == 0 and n_blocks <= ATT_HEAD_DIM
    n_chunks = n_blocks // FLASH_CHUNK
    n_pairs = ATT_HEADS // 2
    pw = 2 * ATT_HEAD_DIM
    cblk = FLASH_CHUNK * MOBA_BLOCK
    qw = jnp.tile(q_norm_w, 2).reshape(pw, 1)
    kw = jnp.tile(k_norm_w, 2).reshape(pw, 1)
    const3 = lambda bi, p: (0, 0, 0)
    vmem = (2 * 3 * s * pw * 4 + 2 * s * pw * 2 + 2 * s * pw * 4 + 6 * s * pw * 2 + s * pw * 4
            + 4 * cblk * MOBA_BLOCK * 4 + (16 << 20))
    y_att, k_t = pl.pallas_call(
        functools.partial(_moba_prompt_kernel, n_blocks=n_blocks),
        grid=(b, n_pairs),
        in_specs=[
            pl.BlockSpec((1, s, pw), lambda bi, p: (bi, 0, COL_Q // pw + p)),
            pl.BlockSpec((1, s, pw), lambda bi, p: (bi, 0, COL_K // pw + p)),
            pl.BlockSpec((1, s, pw), lambda bi, p: (bi, 0, p)),
            pl.BlockSpec((pw, 1), lambda bi, p: (0, 0)),
            pl.BlockSpec((pw, 1), lambda bi, p: (0, 0)),
            pl.BlockSpec((n_blocks, ROPE_HALF, MOBA_BLOCK), const3),
            pl.BlockSpec((n_blocks, ROPE_HALF, MOBA_BLOCK), const3),
        ],
        out_specs=[
            pl.BlockSpec((1, s, pw), lambda bi, p: (bi, 0, p)),
            pl.BlockSpec((1, pw, s), lambda bi, p: (bi, p, 0)),
        ],
        out_shape=[
            jax.ShapeDtypeStruct((b, s, D_ATT), BF16),
            jax.ShapeDtypeStruct((b, D_ATT, s), F32),
        ],
        scratch_shapes=[
            pltpu.VMEM((2, n_blocks, ATT_HEAD_DIM, MOBA_BLOCK), BF16),
            pltpu.VMEM((n_blocks, pw, MOBA_BLOCK), F32),
            pltpu.VMEM((2, n_chunks, cblk, pw), BF16),
            pltpu.VMEM((n_chunks, pw, cblk), BF16),
            pltpu.VMEM((n_blocks, pw, MOBA_BLOCK), BF16),
            pltpu.VMEM((n_blocks, pw), F32),
            pltpu.VMEM((2, 2, cblk, MOBA_BLOCK), F32),
            pltpu.VMEM((2, 1, MOBA_BLOCK), F32),
            pltpu.VMEM((2, 1, MOBA_BLOCK), F32),
            pltpu.VMEM((2, ATT_HEAD_DIM, MOBA_BLOCK), F32),
        ],
        compiler_params=pltpu.CompilerParams(
            dimension_semantics=("arbitrary", "arbitrary"), vmem_limit_bytes=_vmem_limit(vmem)),
        name="moba_prompt",
    )(proj3, proj3, v3, qw, kw, cos_t, sin_t)
    k_heads = k_t.reshape(b, ATT_HEADS, ATT_HEAD_DIM, s).transpose(0, 3, 1, 2)
    return y_att, k_heads


def _seg_sum(x, seg):
    w = x.shape[-1]
    lane = lax.broadcasted_iota(jnp.int32, x.shape, x.ndim - 1)
    s = 1
    while s < seg:
        x = x + jnp.where((lane & s) != 0, pltpu.roll(x, s, x.ndim - 1), pltpu.roll(x, w - s, x.ndim - 1))
        s *= 2
    return x


def _norm_rope_rows(x, w_row, cosf, sin_lo, sin_hi):
    ms = _seg_sum(x * x, ATT_HEAD_DIM) * (1.0 / ATT_HEAD_DIM)
    xn = x * lax.rsqrt(ms + RMS_EPS) * w_row
    width = x.shape[-1]
    return (xn * cosf + pltpu.roll(xn, width - ROPE_HALF, 1) * sin_lo + pltpu.roll(xn, ROPE_HALF, 1) * sin_hi)


def _moba_sample_kernel(pt_ref, q_ref, k_ref, v_ref, qw_ref, kw_ref, cosf_ref, slo_ref, shi_ref, hmask_ref,
                        *rest, T, pages_per_step, n_steps, n_blocks):
    kp = rest[:pages_per_step]
    vp = rest[pages_per_step:2 * pages_per_step]
    y_ref, kout_ref = rest[2 * pages_per_step:2 * pages_per_step + 2]
    qbd_ref, knp_ref, vnp_ref, ms_ref, ls_ref, gs_ref, o_ref = rest[2 * pages_per_step + 2:]
    step = pl.program_id(1)
    rows = ATT_HEADS * T
    scale = ATT_HEAD_DIM ** -0.5
    hmask = hmask_ref[...]

    def fold(o_all):
        o_all = o_all * hmask
        acc = o_all[:, 0:LANES]
        for u in range(1, D_ATT // LANES):
            acc = acc + o_all[:, u * LANES:(u + 1) * LANES]
        return acc

    @pl.when(step == 0)
    def _():
        qn = _norm_rope_rows(q_ref[0], qw_ref[...], cosf_ref[...], slo_ref[...], shi_ref[...])
        kn = _norm_rope_rows(k_ref[0], kw_ref[...], cosf_ref[...], slo_ref[...], shi_ref[...])
        kout_ref[0] = kn
        q_rows = jnp.concatenate([qn * scale] * ATT_HEADS, axis=0) * hmask
        q_hi, q_lo = _split2(q_rows)
        qbd_ref[0:rows, :] = q_hi
        qbd_ref[rows:2 * rows, :] = q_lo
        pad = jnp.zeros((rows - T, D_ATT), F32)
        knp_ref[...] = jnp.concatenate([kn, pad], axis=0).astype(BF16)
        vnp_ref[...] = jnp.concatenate([v_ref[0], pad], axis=0).astype(BF16)
        ms_ref[...] = jnp.full((rows, LANES), NEG_INF, F32)
        gs_ref[...] = jnp.full((rows, LANES), NEG_INF, F32)
        ls_ref[...] = jnp.zeros((rows, LANES), F32)

    lane = lax.broadcasted_iota(jnp.int32, (rows, LANES), 1)
    qbd = qbd_ref[...]
    blocks_per_step = pages_per_step // PAGES_PER_BLOCK
    for jj in range(blocks_per_step):
        j = step * blocks_per_step + jj
        kt = jnp.concatenate([kp[jj * PAGES_PER_BLOCK + u][0, 0].reshape(D_ATT, PAGE_SIZE)
                              for u in range(PAGES_PER_BLOCK)], axis=1).astype(BF16)
        vt = jnp.concatenate([vp[jj * PAGES_PER_BLOCK + u][0, 0].reshape(D_ATT, PAGE_SIZE)
                              for u in range(PAGES_PER_BLOCK)], axis=1).astype(BF16)
        s2 = _dot(qbd, kt)
        s = s2[0:rows, :] + s2[rows:2 * rows, :]
        g_j = jnp.mean(s, axis=1, keepdims=True)
        m_j = jnp.max(s, axis=1, keepdims=True)
        p = jnp.exp(s - m_j)
        l_j = jnp.sum(p, axis=1, keepdims=True)
        o_ref[j] = fold(_dot_nt(p.astype(BF16), vt))
        here = lane == j
        ms_ref[...] = jnp.where(here, m_j, ms_ref[...])
        ls_ref[...] = jnp.where(here, l_j, ls_ref[...])
        gs_ref[...] = jnp.where(here, g_j, gs_ref[...])

    @pl.when(step == n_steps - 1)
    def _():
        r = lax.broadcasted_iota(jnp.int32, (rows, LANES), 0)
        s2 = _dot_nt(qbd, knp_ref[...])
        s_own = s2[0:rows, :] + s2[rows:2 * rows, :]
        s_own = jnp.where((lane < T) & (lane <= (r % T)), s_own, NEG_INF)
        m_o = jnp.max(s_own, axis=1, keepdims=True)
        p_o = jnp.exp(s_own - m_o)
        l_o = jnp.sum(p_o, axis=1, keepdims=True)
        o_o = fold(_dot(p_o.astype(BF16), vnp_ref[...]))

        gs = gs_ref[...]
        cnt = jnp.zeros((rows, LANES), jnp.int32)
        for jp in range(n_blocks):
            gj = gs[:, jp:jp + 1]
            better = (gj > gs) | ((gj == gs) & (jp < lane))
            cnt = cnt + better.astype(jnp.int32)
        sel = (lane < n_blocks) & (cnt < min(MOBA_TOPK, n_blocks))
        ms = ms_ref[...]
        m_tot = jnp.maximum(jnp.max(jnp.where(sel, ms, NEG_INF), axis=1, keepdims=True), m_o)
        w = jnp.where(sel, jnp.exp(ms - m_tot), 0.0)
        w_o = jnp.exp(m_o - m_tot)
        l_tot = jnp.sum(w * ls_ref[...], axis=1, keepdims=True) + w_o * l_o
        acc = w_o * o_o
        for jp in range(n_blocks):
            acc = acc + w[:, jp:jp + 1] * o_ref[jp]
        out = acc / l_tot
        even_head = lax.broadcasted_iota(jnp.int32, (T, LANES), 1) < ATT_HEAD_DIM
        y_ref[0] = jnp.concatenate(
            [jnp.where(even_head, out[2 * hp * T:(2 * hp + 1) * T, :], out[(2 * hp + 1) * T:(2 * hp + 2) * T, :])
             for hp in range(ATT_HEADS // 2)], axis=1).astype(BF16)


def _moba_sample(q3, k3, v3, cache_k, cache_v, page_table, layer, q_norm_w, k_norm_w, cosf, sin_lo, sin_hi):
    nseq, T, _ = v3.shape
    n_pages = page_table.shape[1]
    assert T == SUBLANES and n_pages % PAGES_PER_BLOCK == 0
    n_blocks = n_pages // PAGES_PER_BLOCK
    assert n_blocks <= LANES
    pages_per_step = 8
    assert n_pages % pages_per_step == 0
    n_steps = n_pages // pages_per_step
    rows = ATT_HEADS * T
    hmask = ((jnp.arange(rows)[:, None] // T) == (jnp.arange(D_ATT)[None, :] // ATT_HEAD_DIM)).astype(F32)
    qw = jnp.tile(q_norm_w, ATT_HEADS).reshape(1, D_ATT)
    kw = jnp.tile(k_norm_w, ATT_HEADS).reshape(1, D_ATT)
    ck_t = jnp.transpose(cache_k, (0, 1, 3, 4, 2))
    cv_t = jnp.transpose(cache_v, (0, 1, 3, 4, 2))

    tok = lambda b, s, pt: (b, 0, 0)
    const = lambda b, s, pt: (0, 0)
    page_shape = (1, 1, ATT_HEADS, ATT_HEAD_DIM, PAGE_SIZE)

    def page_spec(u):
        return pl.BlockSpec(page_shape, lambda b, s, pt, u=u: (layer, pt[b, s * pages_per_step + u], 0, 0, 0))

    in_specs = [
        pl.BlockSpec((1, T, D_ATT), lambda b, s, pt: (b, 0, COL_Q // D_ATT)),
        pl.BlockSpec((1, T, D_ATT), lambda b, s, pt: (b, 0, COL_K // D_ATT)),
        pl.BlockSpec((1, T, D_ATT), tok),
        pl.BlockSpec((1, D_ATT), const),
        pl.BlockSpec((1, D_ATT), const),
        pl.BlockSpec((T, D_ATT), const),
        pl.BlockSpec((T, D_ATT), const),
        pl.BlockSpec((T, D_ATT), const),
        pl.BlockSpec((rows, D_ATT), const),
    ] + [page_spec(u) for u in range(pages_per_step)] * 2
    page_vmem = PAGE_SIZE * D_ATT * 4
    vmem = 2 * 2 * pages_per_step * page_vmem + n_blocks * rows * LANES * 4 + (24 << 20)
    return pl.pallas_call(
        functools.partial(_moba_sample_kernel, T=T, pages_per_step=pages_per_step, n_steps=n_steps,
                          n_blocks=n_blocks),
        grid_spec=pltpu.PrefetchScalarGridSpec(
            num_scalar_prefetch=1,
            grid=(nseq, n_steps),
            in_specs=in_specs,
            out_specs=[pl.BlockSpec((1, T, D_ATT), tok), pl.BlockSpec((1, T, D_ATT), tok)],
            scratch_shapes=[
                pltpu.VMEM((2 * rows, D_ATT), BF16),
                pltpu.VMEM((rows, D_ATT), BF16),
                pltpu.VMEM((rows, D_ATT), BF16),
                pltpu.VMEM((rows, LANES), F32),
                pltpu.VMEM((rows, LANES), F32),
                pltpu.VMEM((rows, LANES), F32),
                pltpu.VMEM((n_blocks, rows, LANES), F32),
            ],
        ),
        out_shape=[
            jax.ShapeDtypeStruct((nseq, T, D_ATT), BF16),
            jax.ShapeDtypeStruct((nseq, T, D_ATT), F32),
        ],
        compiler_params=pltpu.CompilerParams(
            dimension_semantics=("arbitrary", "arbitrary"), vmem_limit_bytes=_vmem_limit(vmem)),
        name="moba_sample",
    )(page_table, q3, k3, v3, qw, kw, cosf, sin_lo, sin_hi, hmask,
      *([ck_t] * pages_per_step), *([cv_t] * pages_per_step))


def _merge_kernel(x_ref, yssm_ref, yatt_ref, g_ref, wssm_ref, watt_ref, wout_ref, n2w_ref, x1_ref, h2_ref):
    u_ssm = _dot(yssm_ref[...], wssm_ref[...])
    u_att = _dot(yatt_ref[...], watt_ref[...])
    g = _sigmoid(g_ref[...])
    merged = (g[:, :D_MODEL] * u_ssm + g[:, D_MODEL:] * u_att).astype(BF16)
    x1 = x_ref[...] + _dot(merged, wout_ref[...])
    x1_ref[...] = x1
    ms = jnp.mean(x1 * x1, axis=-1, keepdims=True)
    h2_ref[...] = (x1 * lax.rsqrt(ms + RMS_EPS) * n2w_ref[...]).astype(BF16)


def _merge(x2d, y_ssm, y_att, proj, w_ssm, w_att, w_out, norm2_w, tm):
    n = x2d.shape[0]
    const = lambda i: (0, 0)
    vmem = (2 * tm * (D_MODEL * 4 + D_INNER * 2 + D_ATT * 2 + 2 * D_MODEL * 4 + D_MODEL * 4 + D_MODEL * 2)
            + 2 * 2 * (D_INNER + D_ATT + D_MODEL) * D_MODEL + 8 * tm * D_MODEL * 4 + (4 << 20))
    return pl.pallas_call(
        _merge_kernel,
        grid=(n // tm,),
        in_specs=[
            pl.BlockSpec((tm, D_MODEL), lambda i: (i, 0)),
            pl.BlockSpec((tm, D_INNER), lambda i: (i, 0)),
            pl.BlockSpec((tm, D_ATT), lambda i: (i, 0)),
            pl.BlockSpec((tm, 2 * D_MODEL), lambda i: (i, COL_G // (2 * D_MODEL))),
            pl.BlockSpec((D_INNER, D_MODEL), const),
            pl.BlockSpec((D_ATT, D_MODEL), const),
            pl.BlockSpec((D_MODEL, D_MODEL), const),
            pl.BlockSpec((1, D_MODEL), const),
        ],
        out_specs=[pl.BlockSpec((tm, D_MODEL), lambda i: (i, 0)), pl.BlockSpec((tm, D_MODEL), lambda i: (i, 0))],
        out_shape=[jax.ShapeDtypeStruct((n, D_MODEL), F32), jax.ShapeDtypeStruct((n, D_MODEL), BF16)],
        compiler_params=pltpu.CompilerParams(
            dimension_semantics=("arbitrary",), vmem_limit_bytes=_vmem_limit(vmem)),
        name="merge",
    )(x2d, y_ssm, y_att, proj, w_ssm, w_att, w_out, norm2_w)


def _ffn_kernel(h2_ref, x1_ref, wg_ref, wu_ref, wd_ref, out_ref):
    f = pl.program_id(1)
    h2 = h2_ref[...]
    gate = _dot(h2, wg_ref[...])
    up = _dot(h2, wu_ref[...])
    act = (gate * _sigmoid(gate) * up).astype(BF16)
    part = _dot(act, wd_ref[...])

    @pl.when(f == 0)
    def _():
        out_ref[...] = x1_ref[...] + part

    @pl.when(f != 0)
    def _():
        out_ref[...] = out_ref[...] + part


def _ffn(h2, x1, w_gate, w_up, w_down, tm, tf):
    n = h2.shape[0]
    vmem = (2 * tm * D_MODEL * (2 + 4 + 4) + 2 * 3 * D_MODEL * tf * 2 + 6 * tm * tf * 4 + (4 << 20))
    return pl.pallas_call(
        _ffn_kernel,
        grid=(n // tm, D_FF // tf),
        in_specs=[
            pl.BlockSpec((tm, D_MODEL), lambda i, f: (i, 0)),
            pl.BlockSpec((tm, D_MODEL), lambda i, f: (i, 0)),
            pl.BlockSpec((D_MODEL, tf), lambda i, f: (0, f)),
            pl.BlockSpec((D_MODEL, tf), lambda i, f: (0, f)),
            pl.BlockSpec((tf, D_MODEL), lambda i, f: (f, 0)),
        ],
        out_specs=pl.BlockSpec((tm, D_MODEL), lambda i, f: (i, 0)),
        out_shape=jax.ShapeDtypeStruct((n, D_MODEL), F32),
        compiler_params=pltpu.CompilerParams(
            dimension_semantics=("arbitrary", "arbitrary"), vmem_limit_bytes=_vmem_limit(vmem)),
        name="ffn",
    )(h2, x1, w_gate, w_up, w_down)


def _rope_angles(pos):
    inv_freq = ROPE_THETA ** (-(jnp.arange(ROPE_HALF, dtype=F32) * 2.0 / ROPE_DIM))
    ang = pos.astype(F32)[:, None] * inv_freq[None, :]
    return jnp.cos(ang), jnp.sin(ang)


def _rope_tables_t(pos, n_blocks):
    cos, sin = _rope_angles(pos)
    to_blocks = lambda t: t.T.reshape(ROPE_HALF, n_blocks, MOBA_BLOCK).transpose(1, 0, 2)
    return to_blocks(cos), to_blocks(sin)


def _rope_tables_rows(pos):
    cos, sin = _rope_angles(pos)
    d = jnp.arange(D_ATT) % ATT_HEAD_DIM
    idx = d % ROPE_HALF
    cosf = jnp.where(d[None, :] < ROPE_DIM, cos[:, idx], 1.0)
    sin_lo = jnp.where(d[None, :] < ROPE_HALF, -sin[:, idx], 0.0)
    sin_hi = jnp.where((d[None, :] >= ROPE_HALF) & (d[None, :] < ROPE_DIM), sin[:, idx], 0.0)
    return cosf.astype(F32), sin_lo.astype(F32), sin_hi.astype(F32)


def _layer_weights(l, norm1_w, w_in, conv_w, conv_b, dt_bias, a_log, d_skip, ssm_norm_w, q_norm_w, k_norm_w,
                   w_ssm_branch, w_att_branch, w_out, norm2_w, w_ffn_in, w_ffn_out):
    w = w_in[l]
    off_xbc = D_INNER
    off_dt = off_xbc + CONV_DIM
    off_q = off_dt + SSM_HEADS
    off_k = off_q + D_ATT
    off_v = off_k + D_ATT
    off_g = off_v + D_ATT
    w_main = jnp.concatenate([w[:, off_xbc:off_dt], w[:, off_q:off_k], w[:, 0:off_xbc], w[:, off_g:],
                              w[:, off_k:off_v], w[:, off_v:off_g]], axis=1).astype(BF16)
    w_dt = jnp.pad(w[:, off_dt:off_q], ((0, 0), (0, DT_PAD - SSM_HEADS)))
    wdt_hi = w_dt.astype(BF16)
    wdt_lo = (w_dt - wdt_hi.astype(F32)).astype(BF16)
    pad_h = (0, DT_PAD - SSM_HEADS)
    return dict(
        norm1_w=norm1_w[l].reshape(1, D_MODEL), w_main=w_main, wdt_hi=wdt_hi, wdt_lo=wdt_lo,
        conv_w8=jnp.pad(conv_w[l], ((0, SUBLANES - CONV_WIDTH), (0, 0))), conv_b=conv_b[l].reshape(1, CONV_DIM),
        dt_bias=jnp.pad(dt_bias[l], pad_h).reshape(1, DT_PAD), a_log=jnp.pad(a_log[l], pad_h).reshape(1, DT_PAD),
        d_skip_x=jnp.repeat(d_skip[l], SSM_HEAD_DIM).reshape(1, D_INNER),
        ssm_norm_w=ssm_norm_w[l].reshape(1, D_INNER), q_norm_w=q_norm_w[l], k_norm_w=k_norm_w[l],
        w_ssm=w_ssm_branch[l].astype(BF16), w_att=w_att_branch[l].astype(BF16), w_out=w_out[l].astype(BF16),
        norm2_w=norm2_w[l].reshape(1, D_MODEL),
        w_gate=w_ffn_in[l][:, :D_FF].astype(BF16), w_up=w_ffn_in[l][:, D_FF:].astype(BF16),
        w_down=w_ffn_out[l].astype(BF16),
    )


def _trunk(x, conv_prev, ssm_prev, wts, attend, ssd_chunk, tm, tf):
    b, t, _ = x.shape
    n = b * t
    x2d = x.reshape(n, D_MODEL)
    proj, v, dt = _in_proj(x2d, wts["norm1_w"], wts["w_main"], wts["wdt_hi"], wts["wdt_lo"], tm)
    proj3 = proj.reshape(b, t, PROJ_COLS)
    v3 = v.reshape(b, t, D_ATT)
    conv_prev8 = jnp.pad(conv_prev, ((0, 0), (SUBLANES - (CONV_WIDTH - 1), 0), (0, 0)))
    L, T = ssd_chunk
    y_ssm, ssm_new = _ssd(proj3, dt.reshape(b, t, DT_PAD), conv_prev8, ssm_prev.reshape(b, D_INNER, SSM_STATE),
                          wts["conv_w8"], wts["conv_b"], wts["dt_bias"], wts["a_log"], wts["d_skip_x"],
                          wts["ssm_norm_w"], L, T)
    y_att, k_heads = attend(proj3, v3)
    x1, h2 = _merge(x2d, y_ssm.reshape(n, D_INNER), y_att.reshape(n, D_ATT), proj, wts["w_ssm"], wts["w_att"],
                    wts["w_out"], wts["norm2_w"], tm)
    out = _ffn(h2, x1, wts["w_gate"], wts["w_up"], wts["w_down"], tm, tf)
    conv_new = proj3[:, t - (CONV_WIDTH - 1):, COL_XBC:COL_XBC + CONV_DIM]
    heads = (b, t, ATT_HEADS, ATT_HEAD_DIM)
    return (out.reshape(b, t, D_MODEL), k_heads, v3.reshape(heads), conv_new,
            ssm_new.reshape(b, SSM_HEADS, SSM_HEAD_DIM, SSM_STATE))


def kernel(x_prompt, x_sample, cache_k, cache_v, page_table, state_conv, state_ssm, norm1_w, w_in, conv_w, conv_b, dt_bias, a_log, d_skip, ssm_norm_w, q_norm_w, k_norm_w, w_ssm_branch, w_att_branch, w_out, norm2_w, w_ffn_in, w_ffn_out):
    depth = w_in.shape[0]
    bp, sp, _ = x_prompt.shape
    bs, ts, _ = x_sample.shape
    past_len = page_table.shape[1] * PAGE_SIZE
    assert sp % MOBA_BLOCK == 0 and sp % SSD_CHUNK == 0 and ts <= SSD_CHUNK
    cos_t, sin_t = _rope_tables_t(jnp.arange(sp), sp // MOBA_BLOCK)
    cosf, sin_lo, sin_hi = _rope_tables_rows(past_len + jnp.arange(ts))
    xp, xs = x_prompt, x_sample
    outs = [[] for _ in range(8)]
    for l in range(depth):
        wts = _layer_weights(l, norm1_w, w_in, conv_w, conv_b, dt_bias, a_log, d_skip, ssm_norm_w, q_norm_w,
                             k_norm_w, w_ssm_branch, w_att_branch, w_out, norm2_w, w_ffn_in, w_ffn_out)
        attend_p = lambda proj3, v3: _moba_prompt(proj3, v3, wts["q_norm_w"], wts["k_norm_w"], cos_t, sin_t)
        conv0 = jnp.zeros((bp, CONV_WIDTH - 1, CONV_DIM), F32)
        ssm0 = jnp.zeros((bp, SSM_HEADS, SSM_HEAD_DIM, SSM_STATE), F32)
        xp, kp, vp, cp, ssp = _trunk(xp, conv0, ssm0, wts, attend_p, (SSD_CHUNK, SSD_CHUNK), 512, D_FF // 2)

        def attend_s(proj3, v3):
            y_att, k3 = _moba_sample(proj3, proj3, v3, cache_k, cache_v, page_table, l, wts["q_norm_w"],
                                     wts["k_norm_w"], cosf, sin_lo, sin_hi)
            return y_att, k3.reshape(bs, ts, ATT_HEADS, ATT_HEAD_DIM)

        xs, ksn, vsn, cs, sss = _trunk(xs, state_conv[l], state_ssm[l], wts, attend_s, (LANES, ts), bs * ts,
                                       D_FF // 2)
        for lst, val in zip(outs, (kp, vp, ksn, vsn, cp, cs, ssp, sss)):
            lst.append(val)
    return (xp, xs) + tuple(jnp.stack(o) for o in outs)
```

```python
import functools

import jax
import jax.numpy as jnp
from jax import lax
from jax.experimental import pallas as pl
from jax.experimental.pallas import tpu as pltpu

F32 = jnp.float32
BF16 = jnp.bfloat16

D_MODEL = 1024
D_INNER = 2048
SSM_HEADS = 32
SSM_HEAD_DIM = 64
SSM_GROUPS = 4
SSM_STATE = 128
CONV_WIDTH = 4
CONV_DIM = D_INNER + 2 * SSM_GROUPS * SSM_STATE
SSD_CHUNK = 256
ATT_HEADS = 16
ATT_HEAD_DIM = 64
D_ATT = ATT_HEADS * ATT_HEAD_DIM
ROPE_DIM = 16
ROPE_HALF = ROPE_DIM // 2
ROPE_THETA = 500000.0
MOBA_BLOCK = 256
MOBA_TOPK = 3
PAGE_SIZE = 128
PAGES_PER_BLOCK = MOBA_BLOCK // PAGE_SIZE
D_FF = 2816
RMS_EPS = 1e-6
NEG_INF = float("-inf")
MASK_BIAS = -1e30
LOG2_E = 1.4426950408889634

LANES = 128
SUBLANES = 8
V7X_VMEM_BYTES = 64 * 1024 * 1024

COL_XBC = 0
COL_Q = COL_XBC + CONV_DIM
COL_Z = COL_Q + D_ATT
COL_G = COL_Z + D_INNER
COL_K = COL_G + 2 * D_MODEL
PROJ_COLS = COL_K + D_ATT
PROJ_TILE = 1024
DT_PAD = LANES


def _vmem_limit(nbytes):
    return int(min(nbytes, V7X_VMEM_BYTES - 4 * 1024 * 1024))


def _dot(a, b):
    return jnp.dot(a, b, preferred_element_type=F32)


def _dot_nt(a, b):
    return lax.dot_general(a, b, (((1,), (1,)), ((), ())), preferred_element_type=F32)


def _dot_tn(a, b):
    return lax.dot_general(a, b, (((0,), (0,)), ((), ())), preferred_element_type=F32)


def _split2(x):
    hi = x.astype(BF16)
    lo = (x - hi.astype(F32)).astype(BF16)
    return hi, lo


def _split3(x):
    hi = x.astype(BF16)
    r = x - hi.astype(F32)
    mid = r.astype(BF16)
    lo = (r - mid.astype(F32)).astype(BF16)
    return hi, mid, lo


def _dot_f32(a, b):
    a_hi, a_lo = _split2(a)
    b_hi, b_lo = _split2(b)
    return _dot(a_hi, b_hi) + _dot(a_hi, b_lo) + _dot(a_lo, b_hi)


def _sigmoid(x):
    return 1.0 / (1.0 + jnp.exp(-x))


def _softplus(x):
    return jnp.maximum(x, 0.0) + jnp.log1p(jnp.exp(-jnp.abs(x)))


def _in_proj_kernel(x_ref, nw_ref, w_ref, wdt_hi_ref, wdt_lo_ref, proj_ref, v_ref, dt_ref, h_ref, *, n_col):
    j = pl.program_id(1)

    @pl.when(j == 0)
    def _():
        x = x_ref[...]
        ms = jnp.mean(x * x, axis=-1, keepdims=True)
        h = x * lax.rsqrt(ms + RMS_EPS) * nw_ref[...]
        h_hi, h_lo = _split2(h)
        h_ref[...] = h_hi
        dt_ref[...] = (_dot(h_hi, wdt_hi_ref[...]) + _dot(h_hi, wdt_lo_ref[...])
                       + _dot(h_lo, wdt_hi_ref[...]))

    acc = _dot(h_ref[...], w_ref[...])

    @pl.when(j < n_col - 1)
    def _():
        proj_ref[...] = acc

    @pl.when(j == n_col - 1)
    def _():
        v_ref[...] = acc


def _in_proj(x2d, norm_w, w_main, wdt_hi, wdt_lo, tm):
    n = x2d.shape[0]
    n_col = w_main.shape[1] // PROJ_TILE
    n_proj = PROJ_COLS // PROJ_TILE
    vmem = (2 * tm * D_MODEL * 4 + 2 * D_MODEL * PROJ_TILE * 2 + 4 * tm * PROJ_TILE * 4
            + 2 * tm * DT_PAD * 4 + tm * D_MODEL * 2 + 3 * tm * PROJ_TILE * 4 + (4 << 20))
    return pl.pallas_call(
        functools.partial(_in_proj_kernel, n_col=n_col),
        grid=(n // tm, n_col),
        in_specs=[
            pl.BlockSpec((tm, D_MODEL), lambda i, j: (i, 0)),
            pl.BlockSpec((1, D_MODEL), lambda i, j: (0, 0)),
            pl.BlockSpec((D_MODEL, PROJ_TILE), lambda i, j: (0, j)),
            pl.BlockSpec((D_MODEL, DT_PAD), lambda i, j: (0, 0)),
            pl.BlockSpec((D_MODEL, DT_PAD), lambda i, j: (0, 0)),
        ],
        out_specs=[
            pl.BlockSpec((tm, PROJ_TILE), lambda i, j: (i, jnp.minimum(j, n_proj - 1))),
            pl.BlockSpec((tm, PROJ_TILE), lambda i, j: (i, 0)),
            pl.BlockSpec((tm, DT_PAD), lambda i, j: (i, 0)),
        ],
        out_shape=[
            jax.ShapeDtypeStruct((n, PROJ_COLS), F32),
            jax.ShapeDtypeStruct((n, D_ATT), F32),
            jax.ShapeDtypeStruct((n, DT_PAD), F32),
        ],
        scratch_shapes=[pltpu.VMEM((tm, D_MODEL), BF16)],
        compiler_params=pltpu.CompilerParams(
            dimension_semantics=("arbitrary", "arbitrary"), vmem_limit_bytes=_vmem_limit(vmem)),
        name="in_proj",
    )(x2d, norm_w, w_main, wdt_hi, wdt_lo)


def _ssd_kernel(xbc_ref, z_ref, dt_ref, convp_ref, ssmp_ref, convw_ref, convb_ref, dtb_ref, alog_ref,
                dskip_ref, normw_ref, tri_ref, expand_ref, expand_t_ref,
                y_ref, ssm_out_ref, h_ref, ext_ref, ybuf_ref, *, L, T, n_chunks):
    c = pl.program_id(1)
    gn = SSM_GROUPS * SSM_STATE
    hpg = SSM_HEADS // SSM_GROUPS
    gw = hpg * SSM_HEAD_DIM

    @pl.when(c == 0)
    def _():
        h_ref[...] = ssmp_ref[0]
        ext_ref[0:SUBLANES, :] = convp_ref[0]

    ext_ref[SUBLANES:SUBLANES + T, :] = xbc_ref[0]
    if T < L:
        ext_ref[SUBLANES + T:SUBLANES + L, :] = jnp.zeros((L - T, CONV_DIM), F32)

    cw = convw_ref[...]
    conv = convb_ref[...] + cw[3:4, :] * ext_ref[SUBLANES:SUBLANES + L, :]
    conv = conv + cw[2:3, :] * ext_ref[SUBLANES - 1:SUBLANES - 1 + L, :]
    conv = conv + cw[1:2, :] * ext_ref[SUBLANES - 2:SUBLANES - 2 + L, :]
    conv = conv + cw[0:1, :] * ext_ref[SUBLANES - 3:SUBLANES - 3 + L, :]
    ext_ref[0:SUBLANES, :] = ext_ref[L:L + SUBLANES, :]

    xbc = conv * _sigmoid(conv)
    xs = xbc[:, :D_INNER]
    bm = xbc[:, D_INNER:D_INNER + gn]
    cm = xbc[:, D_INNER + gn:]

    row_l = lax.broadcasted_iota(jnp.int32, (L, DT_PAD), 0)
    if T < L:
        dt_raw = jnp.concatenate([dt_ref[0], jnp.zeros((L - T, DT_PAD), F32)], axis=0)
    else:
        dt_raw = dt_ref[0]
    dt = jnp.where(row_l < T, _softplus(dt_raw + dtb_ref[...]), 0.0)
    a = -jnp.exp(alog_ref[...])
    dta = dt * a
    d_hi, d_mid, d_lo = _split3(dta)
    tri = tri_ref[...]
    acum = _dot(tri, d_hi) + _dot(tri, d_mid) + _dot(tri, d_lo)
    acum_t = acum.T
    last = acum[L - 1:L, :]
    e_last = jnp.exp(last)
    stacked = jnp.concatenate([dt, jnp.exp(acum), jnp.exp(last - acum) * dt], axis=0)
    s_hi, s_lo = _split2(stacked)
    ex = expand_ref[...]
    wide = _dot(s_hi, ex) + _dot(s_lo, ex)
    dt_x = wide[0:L]
    ea_x = wide[L:2 * L]
    elm_x = wide[2 * L:3 * L]
    xdt_b = (xs * dt_x).astype(BF16)
    xdec_b = (xs * elm_x).astype(BF16)

    row = lax.broadcasted_iota(jnp.int32, (L, L), 0)
    col = lax.broadcasted_iota(jnp.int32, (L, L), 1)
    causal = row >= col
    lane = lax.broadcasted_iota(jnp.int32, (L, 2 * SSM_HEAD_DIM), 1)
    first_head = lane < SSM_HEAD_DIM

    for g in range(SSM_GROUPS):
        bg = bm[:, g * SSM_STATE:(g + 1) * SSM_STATE].astype(BF16)
        cg = cm[:, g * SSM_STATE:(g + 1) * SSM_STATE].astype(BF16)
        cb = _dot_nt(cg, bg)
        for pr in range(hpg // 2):
            ha = g * hpg + 2 * pr
            c0 = ha * SSM_HEAD_DIM
            xp = xdt_b[:, c0:c0 + 2 * SSM_HEAD_DIM]
            ys = []
            for hh in (ha, ha + 1):
                dec = jnp.exp(jnp.where(causal, acum[:, hh:hh + 1] - acum_t[hh:hh + 1, :], NEG_INF))
                ys.append(_dot((cb * dec).astype(BF16), xp))
            ybuf_ref[:, c0:c0 + 2 * SSM_HEAD_DIM] = jnp.where(first_head, ys[0], ys[1])
        r0 = g * gw
        hg = h_ref[r0:r0 + gw, :]
        y_state = _dot_nt(cg, hg.astype(BF16)) * ea_x[:, r0:r0 + gw]
        ybuf_ref[:, r0:r0 + gw] = ybuf_ref[:, r0:r0 + gw] + y_state
        upd = _dot_tn(xdec_b[:, r0:r0 + gw], bg)
        el_col = jnp.sum(expand_t_ref[r0:r0 + gw, :] * e_last, axis=1, keepdims=True)
        h_ref[r0:r0 + gw, :] = hg * el_col + upd

    y = ybuf_ref[...] + dskip_ref[...] * xs
    y = y[0:T]
    z = z_ref[0]
    y = y * (z * _sigmoid(z))
    for g in range(SSM_GROUPS):
        yg = y[:, g * gw:(g + 1) * gw]
        ms = jnp.mean(yg * yg, axis=-1, keepdims=True)
        y_ref[0, :, g * gw:(g + 1) * gw] = (yg * lax.rsqrt(ms + RMS_EPS) * normw_ref[:, g * gw:(g + 1) * gw]).astype(BF16)

    @pl.when(c == n_chunks - 1)
    def _():
        ssm_out_ref[0] = h_ref[...]


def _ssd(proj3, dt3, conv_prev8, ssm_prev, conv_w8, conv_b, dt_bias, a_log, d_skip_x, norm_w, L, T):
    b, t_total, _ = proj3.shape
    n_chunks = t_total // T
    tri = (jnp.arange(L)[:, None] >= jnp.arange(L)[None, :]).astype(BF16)
    head_of = jnp.arange(D_INNER) // SSM_HEAD_DIM
    expand = (jnp.arange(DT_PAD)[:, None] == head_of[None, :]).astype(BF16)
    expand_t = (head_of[:, None] == jnp.arange(DT_PAD)[None, :]).astype(F32)
    const = lambda bi, ci: (0, 0)
    vmem = (2 * T * CONV_DIM * 4 + 2 * T * D_INNER * 4 + 4 * D_INNER * SSM_STATE * 4 + (L + SUBLANES) * CONV_DIM * 4
            + L * D_INNER * 4 + 2 * T * D_INNER * 2 + 2 * (DT_PAD * D_INNER * 2 + D_INNER * DT_PAD * 4)
            + 14 * L * D_INNER * 4 + (8 << 20))
    return pl.pallas_call(
        functools.partial(_ssd_kernel, L=L, T=T, n_chunks=n_chunks),
        grid=(b, n_chunks),
        in_specs=[
            pl.BlockSpec((1, T, CONV_DIM), lambda bi, ci: (bi, ci, COL_XBC // CONV_DIM)),
            pl.BlockSpec((1, T, D_INNER), lambda bi, ci: (bi, ci, COL_Z // D_INNER)),
            pl.BlockSpec((1, T, DT_PAD), lambda bi, ci: (bi, ci, 0)),
            pl.BlockSpec((1, SUBLANES, CONV_DIM), lambda bi, ci: (bi, 0, 0)),
            pl.BlockSpec((1, D_INNER, SSM_STATE), lambda bi, ci: (bi, 0, 0)),
            pl.BlockSpec((SUBLANES, CONV_DIM), const),
            pl.BlockSpec((1, CONV_DIM), const),
            pl.BlockSpec((1, DT_PAD), const),
            pl.BlockSpec((1, DT_PAD), const),
            pl.BlockSpec((1, D_INNER), const),
            pl.BlockSpec((1, D_INNER), const),
            pl.BlockSpec((L, L), const),
            pl.BlockSpec((DT_PAD, D_INNER), const),
            pl.BlockSpec((D_INNER, DT_PAD), const),
        ],
        out_specs=[
            pl.BlockSpec((1, T, D_INNER), lambda bi, ci: (bi, ci, 0)),
            pl.BlockSpec((1, D_INNER, SSM_STATE), lambda bi, ci: (bi, 0, 0)),
        ],
        out_shape=[
            jax.ShapeDtypeStruct((b, t_total, D_INNER), BF16),
            jax.ShapeDtypeStruct((b, D_INNER, SSM_STATE), F32),
        ],
        scratch_shapes=[
            pltpu.VMEM((D_INNER, SSM_STATE), F32),
            pltpu.VMEM((L + SUBLANES, CONV_DIM), F32),
            pltpu.VMEM((L, D_INNER), F32),
        ],
        compiler_params=pltpu.CompilerParams(
            dimension_semantics=("arbitrary", "arbitrary"), vmem_limit_bytes=_vmem_limit(vmem)),
        name="ssd",
    )(proj3, proj3, dt3, conv_prev8, ssm_prev, conv_w8, conv_b, dt_bias, a_log, d_skip_x, norm_w,
      tri, expand, expand_t)


FLASH_CHUNK = 4


def _norm_rope_t(x_t, w_col, cos, sin):
    outs = []
    for h in range(2):
        xh = x_t[h * ATT_HEAD_DIM:(h + 1) * ATT_HEAD_DIM, :]
        ms = jnp.mean(xh * xh, axis=0, keepdims=True)
        xn = xh * lax.rsqrt(ms + RMS_EPS) * w_col[h * ATT_HEAD_DIM:(h + 1) * ATT_HEAD_DIM, :]
        x1 = xn[0:ROPE_HALF, :]
        x2 = xn[ROPE_HALF:ROPE_DIM, :]
        outs += [x1 * cos - x2 * sin, x2 * cos + x1 * sin, xn[ROPE_DIM:, :]]
    return jnp.concatenate(outs, axis=0)


def _moba_prompt_kernel(q_ref, k_ref, v_ref, qw_ref, kw_ref, cos_ref, sin_ref, y_ref, kout_ref, vout_ref,
                        qt_ref, qtf_ref, kn_ref, vtc_ref, vtb_ref, km_ref, s_ref, mx_ref, m_ref, l_ref, acc_ref,
                        *, n_blocks):
    blk = MOBA_BLOCK
    cw = FLASH_CHUNK
    n_chunks = n_blocks // cw
    hd = ATT_HEAD_DIM
    pair_w = 2 * hd
    scale = (ATT_HEAD_DIM ** -0.5) * LOG2_E
    lane_k = lax.broadcasted_iota(jnp.int32, (blk, pair_w), 1)

    for i in range(n_blocks):
        r0 = i * blk
        c0 = (i % cw) * blk
        cos = cos_ref[i]
        sin = sin_ref[i]
        qn = _norm_rope_t(q_ref[0, r0:r0 + blk, :].T, qw_ref[...], cos, sin)
        qtf_ref[i] = qn
        qs = (qn * scale).astype(BF16)
        qt_ref[0, i, 0:hd, :] = qs[0:hd, :]
        qt_ref[1, i, 0:hd, :] = qs[hd:pair_w, :]
        kn_t = _norm_rope_t(k_ref[0, r0:r0 + blk, :].T, kw_ref[...], cos, sin)
        kout_ref[0, :, r0:r0 + blk] = kn_t
        kn = kn_t.T
        km_ref[i:i + 1, :] = jnp.mean(kn, axis=0, keepdims=True)
        onehot = (lane_k == hd + i).astype(F32)
        kn_ref[0, i // cw, c0:c0 + blk, :] = jnp.where(lane_k < hd, kn, onehot).astype(BF16)
        kn_ref[1, i // cw, c0:c0 + blk, :] = jnp.where(lane_k < hd, pltpu.roll(kn, hd, 1), onehot).astype(BF16)
        vt = v_ref[0, r0:r0 + blk, :].T
        vout_ref[0, :, r0:r0 + blk] = vt
        vt = vt.astype(BF16)
        vtb_ref[i] = vt
        vtc_ref[i // cw, :, c0:c0 + blk] = vt

    km = km_ref[...]
    km_lane = lax.broadcasted_iota(jnp.int32, km.shape, 1)
    km_heads = (jnp.where(km_lane < hd, km, 0.0), jnp.where(km_lane < hd, 0.0, km))
    jrow = lax.broadcasted_iota(jnp.int32, (n_blocks, blk), 0)
    k_sel = min(MOBA_TOPK, n_blocks)
    pad_rows = jnp.zeros((hd - n_blocks, blk), BF16)

    for i in range(n_blocks):
        qtf = qtf_ref[i]
        for h in range(2):
            gate = _dot_f32(km_heads[h], qtf)
            valid = jrow < i
            gate = jnp.where(valid, gate, NEG_INF)
            cnt = jnp.zeros((n_blocks, blk), jnp.int32)
            for jp in range(i):
                gj = gate[jp:jp + 1, :]
                better = (gj > gate) | ((gj == gate) & (jp < jrow))
                cnt = cnt + better.astype(jnp.int32)
            sel = valid & (cnt < k_sel)
            bias = jnp.where(sel, 0.0, MASK_BIAS).astype(BF16)
            qt_ref[h, i, hd:pair_w, :] = jnp.concatenate([bias, pad_rows], axis=0)

    key_i = lax.broadcasted_iota(jnp.int32, (blk, blk), 0)
    qry_i = lax.broadcasted_iota(jnp.int32, (blk, blk), 1)
    no_bias = jnp.zeros((hd, blk), BF16)
    mx_ref[...] = jnp.full(mx_ref.shape, NEG_INF, F32)

    def chunk_scores(c, buf, h, qt):
        for u in range(cw):
            t = _dot(kn_ref[h, c, u * blk:(u + 1) * blk, :], qt)
            s_ref[buf, h, u * blk:(u + 1) * blk, :] = t
            mx_ref[buf, h, u:u + 1, :] = jnp.max(t, axis=0, keepdims=True)

    def attend(i, carry):
        r0 = pl.multiple_of(i * blk, blk)
        ic = lax.div(i, cw)
        c0 = pl.multiple_of(lax.rem(i, cw) * blk, blk)
        vd = vtb_ref[i]
        qts = []
        for h in range(2):
            qts.append(qt_ref[h, i])
            qd = qts[h][0:hd, :]

            s = _dot(kn_ref[h, ic, pl.ds(c0, blk), :], jnp.concatenate([qd, no_bias], axis=0))
            s = jnp.where(key_i <= qry_i, s, NEG_INF)
            m = jnp.max(s, axis=0, keepdims=True)
            p = jnp.exp2(s - m)
            m_ref[h] = m
            l_ref[h] = jnp.sum(p, axis=0, keepdims=True)
            acc_ref[h] = _dot(vd[h * hd:(h + 1) * hd, :], p.astype(BF16))
            chunk_scores(0, 0, h, qts[h])

        for c in range(n_chunks):
            @pl.when(c * cw < i)
            def _(c=c):
                slot = c % 2
                if c + 1 < n_chunks:
                    for h in range(2):
                        chunk_scores(c + 1, 1 - slot, h, qts[h])
                for h in range(2):
                    s = s_ref[slot, h]
                    m0 = m_ref[h]
                    m1 = jnp.maximum(m0, jnp.max(mx_ref[slot, h], axis=0, keepdims=True))
                    alpha = jnp.exp2(m0 - m1)
                    p = jnp.exp2(s - m1)
                    m_ref[h] = m1
                    l_ref[h] = alpha * l_ref[h] + jnp.sum(p, axis=0, keepdims=True)
                    acc_ref[h] = alpha * acc_ref[h] + _dot(vtc_ref[c, h * hd:(h + 1) * hd, :], p.astype(BF16))

        o_t = jnp.concatenate([acc_ref[0] / l_ref[0], acc_ref[1] / l_ref[1]], axis=0)
        y_ref[0, pl.ds(r0, blk), :] = o_t.T.astype(BF16)
        return carry

    lax.fori_loop(0, n_blocks, attend, 0)


def _moba_prompt(proj3, v3, q_norm_w, k_norm_w, cos_t, sin_t):
    b, s, _ = proj3.shape
    n_blocks = s // MOBA_BLOCK
    assert n_blocks % FLASH_CHUNK == 0 and n_blocks <= ATT_HEAD_DIM
    n_chunks = n_blocks // FLASH_CHUNK
    n_pairs = ATT_HEADS // 2
    pw = 2 * ATT_HEAD_DIM
    cblk = FLASH_CHUNK * MOBA_BLOCK
    qw = jnp.tile(q_norm_w, 2).reshape(pw, 1)
    kw = jnp.tile(k_norm_w, 2).reshape(pw, 1)
    const3 = lambda bi, p: (0, 0, 0)
    vmem = (2 * 3 * s * pw * 4 + 2 * s * pw * 2 + 4 * s * pw * 4 + 7 * s * pw * 2 + s * pw * 4
            + 4 * cblk * MOBA_BLOCK * 4 + (16 << 20))
    y_att, k_t, v_t = pl.pallas_call(
        functools.partial(_moba_prompt_kernel, n_blocks=n_blocks),
        grid=(b, n_pairs),
        in_specs=[
            pl.BlockSpec((1, s, pw), lambda bi, p: (bi, 0, COL_Q // pw + p)),
            pl.BlockSpec((1, s, pw), lambda bi, p: (bi, 0, COL_K // pw + p)),
            pl.BlockSpec((1, s, pw), lambda bi, p: (bi, 0, p)),
            pl.BlockSpec((pw, 1), lambda bi, p: (0, 0)),
            pl.BlockSpec((pw, 1), lambda bi, p: (0, 0)),
            pl.BlockSpec((n_blocks, ROPE_HALF, MOBA_BLOCK), const3),
            pl.BlockSpec((n_blocks, ROPE_HALF, MOBA_BLOCK), const3),
        ],
        out_specs=[
            pl.BlockSpec((1, s, pw), lambda bi, p: (bi, 0, p)),
            pl.BlockSpec((1, pw, s), lambda bi, p: (bi, p, 0)),
            pl.BlockSpec((1, pw, s), lambda bi, p: (bi, p, 0)),
        ],
        out_shape=[
            jax.ShapeDtypeStruct((b, s, D_ATT), BF16),
            jax.ShapeDtypeStruct((b, D_ATT, s), F32),
            jax.ShapeDtypeStruct((b, D_ATT, s), F32),
        ],
        scratch_shapes=[
            pltpu.VMEM((2, n_blocks, pw, MOBA_BLOCK), BF16),
            pltpu.VMEM((n_blocks, pw, MOBA_BLOCK), F32),
            pltpu.VMEM((2, n_chunks, cblk, pw), BF16),
            pltpu.VMEM((n_chunks, pw, cblk), BF16),
            pltpu.VMEM((n_blocks, pw, MOBA_BLOCK), BF16),
            pltpu.VMEM((n_blocks, pw), F32),
            pltpu.VMEM((2, 2, cblk, MOBA_BLOCK), F32),
            pltpu.VMEM((2, 2, SUBLANES, MOBA_BLOCK), F32),
            pltpu.VMEM((2, 1, MOBA_BLOCK), F32),
            pltpu.VMEM((2, 1, MOBA_BLOCK), F32),
            pltpu.VMEM((2, ATT_HEAD_DIM, MOBA_BLOCK), F32),
        ],
        compiler_params=pltpu.CompilerParams(
            dimension_semantics=("arbitrary", "arbitrary"), vmem_limit_bytes=_vmem_limit(vmem)),
        name="moba_prompt",
    )(proj3, proj3, v3, qw, kw, cos_t, sin_t)
    to_heads = lambda t: t.reshape(b, ATT_HEADS, ATT_HEAD_DIM, s).transpose(0, 3, 1, 2)
    return y_att, to_heads(k_t), to_heads(v_t)


def _seg_sum(x, seg):
    w = x.shape[-1]
    lane = lax.broadcasted_iota(jnp.int32, x.shape, x.ndim - 1)
    s = 1
    while s < seg:
        x = x + jnp.where((lane & s) != 0, pltpu.roll(x, s, x.ndim - 1), pltpu.roll(x, w - s, x.ndim - 1))
        s *= 2
    return x


def _norm_rope_rows(x, w_row, cosf, sin_lo, sin_hi):
    ms = _seg_sum(x * x, ATT_HEAD_DIM) * (1.0 / ATT_HEAD_DIM)
    xn = x * lax.rsqrt(ms + RMS_EPS) * w_row
    width = x.shape[-1]
    return (xn * cosf + pltpu.roll(xn, width - ROPE_HALF, 1) * sin_lo + pltpu.roll(xn, ROPE_HALF, 1) * sin_hi)


def _moba_sample_kernel(pt_ref, q_ref, k_ref, v_ref, qw_ref, kw_ref, cosf_ref, slo_ref, shi_ref, hmask_ref,
                        *rest, T, pages_per_step, n_steps, n_blocks):
    kp = rest[:pages_per_step]
    vp = rest[pages_per_step:2 * pages_per_step]
    y_ref, kout_ref = rest[2 * pages_per_step:2 * pages_per_step + 2]
    qbd_ref, knp_ref, vnp_ref, ms_ref, ls_ref, gs_ref, o_ref = rest[2 * pages_per_step + 2:]
    step = pl.program_id(1)
    rows = ATT_HEADS * T
    scale = ATT_HEAD_DIM ** -0.5
    hmask = hmask_ref[...]

    def fold(o_all):
        o_all = o_all * hmask
        acc = o_all[:, 0:LANES]
        for u in range(1, D_ATT // LANES):
            acc = acc + o_all[:, u * LANES:(u + 1) * LANES]
        return acc

    @pl.when(step == 0)
    def _():
        qn = _norm_rope_rows(q_ref[0], qw_ref[...], cosf_ref[...], slo_ref[...], shi_ref[...])
        kn = _norm_rope_rows(k_ref[0], kw_ref[...], cosf_ref[...], slo_ref[...], shi_ref[...])
        kout_ref[0] = kn
        q_rows = jnp.concatenate([qn * scale] * ATT_HEADS, axis=0) * hmask
        q_hi, q_lo = _split2(q_rows)
        qbd_ref[0:rows, :] = q_hi
        qbd_ref[rows:2 * rows, :] = q_lo
        pad = jnp.zeros((rows - T, D_ATT), F32)
        knp_ref[...] = jnp.concatenate([kn, pad], axis=0).astype(BF16)
        vnp_ref[...] = jnp.concatenate([v_ref[0], pad], axis=0).astype(BF16)
        ms_ref[...] = jnp.full((rows, LANES), NEG_INF, F32)
        gs_ref[...] = jnp.full((rows, LANES), NEG_INF, F32)
        ls_ref[...] = jnp.zeros((rows, LANES), F32)

    lane = lax.broadcasted_iota(jnp.int32, (rows, LANES), 1)
    qbd = qbd_ref[...]
    blocks_per_step = pages_per_step // PAGES_PER_BLOCK
    for jj in range(blocks_per_step):
        j = step * blocks_per_step + jj
        kt = jnp.concatenate([kp[jj * PAGES_PER_BLOCK + u][0, 0].reshape(D_ATT, PAGE_SIZE)
                              for u in range(PAGES_PER_BLOCK)], axis=1).astype(BF16)
        vt = jnp.concatenate([vp[jj * PAGES_PER_BLOCK + u][0, 0].reshape(D_ATT, PAGE_SIZE)
                              for u in range(PAGES_PER_BLOCK)], axis=1).astype(BF16)
        s2 = _dot(qbd, kt)
        s = s2[0:rows, :] + s2[rows:2 * rows, :]
        g_j = jnp.mean(s, axis=1, keepdims=True)
        m_j = jnp.max(s, axis=1, keepdims=True)
        p = jnp.exp(s - m_j)
        l_j = jnp.sum(p, axis=1, keepdims=True)
        o_ref[j] = fold(_dot_nt(p.astype(BF16), vt))
        here = lane == j
        ms_ref[...] = jnp.where(here, m_j, ms_ref[...])
        ls_ref[...] = jnp.where(here, l_j, ls_ref[...])
        gs_ref[...] = jnp.where(here, g_j, gs_ref[...])

    @pl.when(step == n_steps - 1)
    def _():
        r = lax.broadcasted_iota(jnp.int32, (rows, LANES), 0)
        s2 = _dot_nt(qbd, knp_ref[...])
        s_own = s2[0:rows, :] + s2[rows:2 * rows, :]
        s_own = jnp.where((lane < T) & (lane <= (r % T)), s_own, NEG_INF)
        m_o = jnp.max(s_own, axis=1, keepdims=True)
        p_o = jnp.exp(s_own - m_o)
        l_o = jnp.sum(p_o, axis=1, keepdims=True)
        o_o = fold(_dot(p_o.astype(BF16), vnp_ref[...]))

        gs = gs_ref[...]
        cnt = jnp.zeros((rows, LANES), jnp.int32)
        for jp in range(n_blocks):
            gj = gs[:, jp:jp + 1]
            better = (gj > gs) | ((gj == gs) & (jp < lane))
            cnt = cnt + better.astype(jnp.int32)
        sel = (lane < n_blocks) & (cnt < min(MOBA_TOPK, n_blocks))
        ms = ms_ref[...]
        m_tot = jnp.maximum(jnp.max(jnp.where(sel, ms, NEG_INF), axis=1, keepdims=True), m_o)
        w = jnp.where(sel, jnp.exp(ms - m_tot), 0.0)
        w_o = jnp.exp(m_o - m_tot)
        l_tot = jnp.sum(w * ls_ref[...], axis=1, keepdims=True) + w_o * l_o
        acc = w_o * o_o
        for jp in range(n_blocks):
            acc = acc + w[:, jp:jp + 1] * o_ref[jp]
        out = acc / l_tot
        even_head = lax.broadcasted_iota(jnp.int32, (T, LANES), 1) < ATT_HEAD_DIM
        y_ref[0] = jnp.concatenate(
            [jnp.where(even_head, out[2 * hp * T:(2 * hp + 1) * T, :], out[(2 * hp + 1) * T:(2 * hp + 2) * T, :])
             for hp in range(ATT_HEADS // 2)], axis=1).astype(BF16)


def _moba_sample(q3, k3, v3, cache_k, cache_v, page_table, layer, q_norm_w, k_norm_w, cosf, sin_lo, sin_hi):
    nseq, T, _ = v3.shape
    n_pages = page_table.shape[1]
    assert T == SUBLANES and n_pages % PAGES_PER_BLOCK == 0
    n_blocks = n_pages // PAGES_PER_BLOCK
    assert n_blocks <= LANES
    pages_per_step = 16 if n_pages % 16 == 0 else PAGES_PER_BLOCK
    assert n_pages % pages_per_step == 0
    n_steps = n_pages // pages_per_step
    rows = ATT_HEADS * T
    hmask = ((jnp.arange(rows)[:, None] // T) == (jnp.arange(D_ATT)[None, :] // ATT_HEAD_DIM)).astype(F32)
    qw = jnp.tile(q_norm_w, ATT_HEADS).reshape(1, D_ATT)
    kw = jnp.tile(k_norm_w, ATT_HEADS).reshape(1, D_ATT)
    ck_t = jnp.transpose(cache_k, (0, 1, 3, 4, 2))
    cv_t = jnp.transpose(cache_v, (0, 1, 3, 4, 2))

    tok = lambda b, s, pt: (b, 0, 0)
    const = lambda b, s, pt: (0, 0)
    page_shape = (1, 1, ATT_HEADS, ATT_HEAD_DIM, PAGE_SIZE)

    def page_spec(u):
        return pl.BlockSpec(page_shape, lambda b, s, pt, u=u: (layer, pt[b, s * pages_per_step + u], 0, 0, 0))

    in_specs = [
        pl.BlockSpec((1, T, D_ATT), lambda b, s, pt: (b, 0, COL_Q // D_ATT)),
        pl.BlockSpec((1, T, D_ATT), lambda b, s, pt: (b, 0, COL_K // D_ATT)),
        pl.BlockSpec((1, T, D_ATT), tok),
        pl.BlockSpec((1, D_ATT), const),
        pl.BlockSpec((1, D_ATT), const),
        pl.BlockSpec((T, D_ATT), const),
        pl.BlockSpec((T, D_ATT), const),
        pl.BlockSpec((T, D_ATT), const),
        pl.BlockSpec((rows, D_ATT), const),
    ] + [page_spec(u) for u in range(pages_per_step)] * 2
    page_vmem = PAGE_SIZE * D_ATT * 4
    vmem = 2 * 2 * pages_per_step * page_vmem + n_blocks * rows * LANES * 4 + (24 << 20)
    return pl.pallas_call(
        functools.partial(_moba_sample_kernel, T=T, pages_per_step=pages_per_step, n_steps=n_steps,
                          n_blocks=n_blocks),
        grid_spec=pltpu.PrefetchScalarGridSpec(
            num_scalar_prefetch=1,
            grid=(nseq, n_steps),
            in_specs=in_specs,
            out_specs=[pl.BlockSpec((1, T, D_ATT), tok), pl.BlockSpec((1, T, D_ATT), tok)],
            scratch_shapes=[
                pltpu.VMEM((2 * rows, D_ATT), BF16),
                pltpu.VMEM((rows, D_ATT), BF16),
                pltpu.VMEM((rows, D_ATT), BF16),
                pltpu.VMEM((rows, LANES), F32),
                pltpu.VMEM((rows, LANES), F32),
                pltpu.VMEM((rows, LANES), F32),
                pltpu.VMEM((n_blocks, rows, LANES), F32),
            ],
        ),
        out_shape=[
            jax.ShapeDtypeStruct((nseq, T, D_ATT), BF16),
            jax.ShapeDtypeStruct((nseq, T, D_ATT), F32),
        ],
        compiler_params=pltpu.CompilerParams(
            dimension_semantics=("arbitrary", "arbitrary"), vmem_limit_bytes=_vmem_limit(vmem)),
        name="moba_sample",
    )(page_table, q3, k3, v3, qw, kw, cosf, sin_lo, sin_hi, hmask,
      *([ck_t] * pages_per_step), *([cv_t] * pages_per_step))


def _merge_kernel(x_ref, yssm_ref, yatt_ref, g_ref, wssm_ref, watt_ref, wout_ref, n2w_ref, x1_ref, h2_ref):
    u_ssm = _dot(yssm_ref[...], wssm_ref[...])
    u_att = _dot(yatt_ref[...], watt_ref[...])
    g = _sigmoid(g_ref[...])
    merged = (g[:, :D_MODEL] * u_ssm + g[:, D_MODEL:] * u_att).astype(BF16)
    x1 = x_ref[...] + _dot(merged, wout_ref[...])
    x1_ref[...] = x1
    ms = jnp.mean(x1 * x1, axis=-1, keepdims=True)
    h2_ref[...] = (x1 * lax.rsqrt(ms + RMS_EPS) * n2w_ref[...]).astype(BF16)


def _merge(x2d, y_ssm, y_att, proj, w_ssm, w_att, w_out, norm2_w, tm):
    n = x2d.shape[0]
    const = lambda i: (0, 0)
    vmem = (2 * tm * (D_MODEL * 4 + D_INNER * 2 + D_ATT * 2 + 2 * D_MODEL * 4 + D_MODEL * 4 + D_MODEL * 2)
            + 2 * 2 * (D_INNER + D_ATT + D_MODEL) * D_MODEL + 8 * tm * D_MODEL * 4 + (4 << 20))
    return pl.pallas_call(
        _merge_kernel,
        grid=(n // tm,),
        in_specs=[
            pl.BlockSpec((tm, D_MODEL), lambda i: (i, 0)),
            pl.BlockSpec((tm, D_INNER), lambda i: (i, 0)),
            pl.BlockSpec((tm, D_ATT), lambda i: (i, 0)),
            pl.BlockSpec((tm, 2 * D_MODEL), lambda i: (i, COL_G // (2 * D_MODEL))),
            pl.BlockSpec((D_INNER, D_MODEL), const),
            pl.BlockSpec((D_ATT, D_MODEL), const),
            pl.BlockSpec((D_MODEL, D_MODEL), const),
            pl.BlockSpec((1, D_MODEL), const),
        ],
        out_specs=[pl.BlockSpec((tm, D_MODEL), lambda i: (i, 0)), pl.BlockSpec((tm, D_MODEL), lambda i: (i, 0))],
        out_shape=[jax.ShapeDtypeStruct((n, D_MODEL), F32), jax.ShapeDtypeStruct((n, D_MODEL), BF16)],
        compiler_params=pltpu.CompilerParams(
            dimension_semantics=("arbitrary",), vmem_limit_bytes=_vmem_limit(vmem)),
        name="merge",
    )(x2d, y_ssm, y_att, proj, w_ssm, w_att, w_out, norm2_w)


def _ffn_kernel(h2_ref, x1_ref, wg_ref, wu_ref, wd_ref, out_ref):
    f = pl.program_id(1)
    h2 = h2_ref[...]
    gate = _dot(h2, wg_ref[...])
    up = _dot(h2, wu_ref[...])
    act = (gate * _sigmoid(gate) * up).astype(BF16)
    part = _dot(act, wd_ref[...])

    @pl.when(f == 0)
    def _():
        out_ref[...] = x1_ref[...] + part

    @pl.when(f != 0)
    def _():
        out_ref[...] = out_ref[...] + part


def _ffn(h2, x1, w_gate, w_up, w_down, tm, tf):
    n = h2.shape[0]
    vmem = (2 * tm * D_MODEL * (2 + 4 + 4) + 2 * 3 * D_MODEL * tf * 2 + 6 * tm * tf * 4 + (4 << 20))
    return pl.pallas_call(
        _ffn_kernel,
        grid=(n // tm, D_FF // tf),
        in_specs=[
            pl.BlockSpec((tm, D_MODEL), lambda i, f: (i, 0)),
            pl.BlockSpec((tm, D_MODEL), lambda i, f: (i, 0)),
            pl.BlockSpec((D_MODEL, tf), lambda i, f: (0, f)),
            pl.BlockSpec((D_MODEL, tf), lambda i, f: (0, f)),
            pl.BlockSpec((tf, D_MODEL), lambda i, f: (f, 0)),
        ],
        out_specs=pl.BlockSpec((tm, D_MODEL), lambda i, f: (i, 0)),
        out_shape=jax.ShapeDtypeStruct((n, D_MODEL), F32),
        compiler_params=pltpu.CompilerParams(
            dimension_semantics=("arbitrary", "arbitrary"), vmem_limit_bytes=_vmem_limit(vmem)),
        name="ffn",
    )(h2, x1, w_gate, w_up, w_down)


def _rope_angles(pos):
    inv_freq = ROPE_THETA ** (-(jnp.arange(ROPE_HALF, dtype=F32) * 2.0 / ROPE_DIM))
    ang = pos.astype(F32)[:, None] * inv_freq[None, :]
    return jnp.cos(ang), jnp.sin(ang)


def _rope_tables_t(pos, n_blocks):
    cos, sin = _rope_angles(pos)
    to_blocks = lambda t: t.T.reshape(ROPE_HALF, n_blocks, MOBA_BLOCK).transpose(1, 0, 2)
    return to_blocks(cos), to_blocks(sin)


def _rope_tables_rows(pos):
    cos, sin = _rope_angles(pos)
    d = jnp.arange(D_ATT) % ATT_HEAD_DIM
    idx = d % ROPE_HALF
    cosf = jnp.where(d[None, :] < ROPE_DIM, cos[:, idx], 1.0)
    sin_lo = jnp.where(d[None, :] < ROPE_HALF, -sin[:, idx], 0.0)
    sin_hi = jnp.where((d[None, :] >= ROPE_HALF) & (d[None, :] < ROPE_DIM), sin[:, idx], 0.0)
    return cosf.astype(F32), sin_lo.astype(F32), sin_hi.astype(F32)


def _layer_weights(l, norm1_w, w_in, conv_w, conv_b, dt_bias, a_log, d_skip, ssm_norm_w, q_norm_w, k_norm_w,
                   w_ssm_branch, w_att_branch, w_out, norm2_w, w_ffn_in, w_ffn_out):
    w = w_in[l]
    off_xbc = D_INNER
    off_dt = off_xbc + CONV_DIM
    off_q = off_dt + SSM_HEADS
    off_k = off_q + D_ATT
    off_v = off_k + D_ATT
    off_g = off_v + D_ATT
    w_main = jnp.concatenate([w[:, off_xbc:off_dt], w[:, off_q:off_k], w[:, 0:off_xbc], w[:, off_g:],
                              w[:, off_k:off_v], w[:, off_v:off_g]], axis=1).astype(BF16)
    w_dt = jnp.pad(w[:, off_dt:off_q], ((0, 0), (0, DT_PAD - SSM_HEADS)))
    wdt_hi = w_dt.astype(BF16)
    wdt_lo = (w_dt - wdt_hi.astype(F32)).astype(BF16)
    pad_h = (0, DT_PAD - SSM_HEADS)
    return dict(
        norm1_w=norm1_w[l].reshape(1, D_MODEL), w_main=w_main, wdt_hi=wdt_hi, wdt_lo=wdt_lo,
        conv_w8=jnp.pad(conv_w[l], ((0, SUBLANES - CONV_WIDTH), (0, 0))), conv_b=conv_b[l].reshape(1, CONV_DIM),
        dt_bias=jnp.pad(dt_bias[l], pad_h).reshape(1, DT_PAD), a_log=jnp.pad(a_log[l], pad_h).reshape(1, DT_PAD),
        d_skip_x=jnp.repeat(d_skip[l], SSM_HEAD_DIM).reshape(1, D_INNER),
        ssm_norm_w=ssm_norm_w[l].reshape(1, D_INNER), q_norm_w=q_norm_w[l], k_norm_w=k_norm_w[l],
        w_ssm=w_ssm_branch[l].astype(BF16), w_att=w_att_branch[l].astype(BF16), w_out=w_out[l].astype(BF16),
        norm2_w=norm2_w[l].reshape(1, D_MODEL),
        w_gate=w_ffn_in[l][:, :D_FF].astype(BF16), w_up=w_ffn_in[l][:, D_FF:].astype(BF16),
        w_down=w_ffn_out[l].astype(BF16),
    )


def _trunk(x, conv_prev, ssm_prev, wts, attend, ssd_chunk, tm_proj, tm, tf):
    b, t, _ = x.shape
    n = b * t
    x2d = x.reshape(n, D_MODEL)
    proj, v, dt = _in_proj(x2d, wts["norm1_w"], wts["w_main"], wts["wdt_hi"], wts["wdt_lo"], tm_proj)
    proj3 = proj.reshape(b, t, PROJ_COLS)
    v3 = v.reshape(b, t, D_ATT)
    conv_prev8 = jnp.pad(conv_prev, ((0, 0), (SUBLANES - (CONV_WIDTH - 1), 0), (0, 0)))
    L, T = ssd_chunk
    y_ssm, ssm_new = _ssd(proj3, dt.reshape(b, t, DT_PAD), conv_prev8, ssm_prev.reshape(b, D_INNER, SSM_STATE),
                          wts["conv_w8"], wts["conv_b"], wts["dt_bias"], wts["a_log"], wts["d_skip_x"],
                          wts["ssm_norm_w"], L, T)
    y_att, k_heads, v_heads = attend(proj3, v3)
    x1, h2 = _merge(x2d, y_ssm.reshape(n, D_INNER), y_att.reshape(n, D_ATT), proj, wts["w_ssm"], wts["w_att"],
                    wts["w_out"], wts["norm2_w"], tm)
    out = _ffn(h2, x1, wts["w_gate"], wts["w_up"], wts["w_down"], tm, tf)
    conv_new = proj3[:, t - (CONV_WIDTH - 1):, COL_XBC:COL_XBC + CONV_DIM]
    return (out.reshape(b, t, D_MODEL), k_heads, v_heads, conv_new,
            ssm_new.reshape(b, SSM_HEADS, SSM_HEAD_DIM, SSM_STATE))


def kernel(x_prompt, x_sample, cache_k, cache_v, page_table, state_conv, state_ssm, norm1_w, w_in, conv_w, conv_b, dt_bias, a_log, d_skip, ssm_norm_w, q_norm_w, k_norm_w, w_ssm_branch, w_att_branch, w_out, norm2_w, w_ffn_in, w_ffn_out):
    depth = w_in.shape[0]
    bp, sp, _ = x_prompt.shape
    bs, ts, _ = x_sample.shape
    past_len = page_table.shape[1] * PAGE_SIZE
    assert sp % MOBA_BLOCK == 0 and sp % SSD_CHUNK == 0 and ts <= SSD_CHUNK
    cos_t, sin_t = _rope_tables_t(jnp.arange(sp), sp // MOBA_BLOCK)
    cosf, sin_lo, sin_hi = _rope_tables_rows(past_len + jnp.arange(ts))
    xp, xs = x_prompt, x_sample
    outs = [[] for _ in range(8)]
    for l in range(depth):
        wts = _layer_weights(l, norm1_w, w_in, conv_w, conv_b, dt_bias, a_log, d_skip, ssm_norm_w, q_norm_w,
                             k_norm_w, w_ssm_branch, w_att_branch, w_out, norm2_w, w_ffn_in, w_ffn_out)
        attend_p = lambda proj3, v3: _moba_prompt(proj3, v3, wts["q_norm_w"], wts["k_norm_w"], cos_t, sin_t)
        conv0 = jnp.zeros((bp, CONV_WIDTH - 1, CONV_DIM), F32)
        ssm0 = jnp.zeros((bp, SSM_HEADS, SSM_HEAD_DIM, SSM_STATE), F32)
        xp, kp, vp, cp, ssp = _trunk(xp, conv0, ssm0, wts, attend_p, (SSD_CHUNK, SSD_CHUNK), min(1024, bp * sp),
                                     512, D_FF // 2)

        def attend_s(proj3, v3):
            y_att, k3 = _moba_sample(proj3, proj3, v3, cache_k, cache_v, page_table, l, wts["q_norm_w"],
                                     wts["k_norm_w"], cosf, sin_lo, sin_hi)
            heads = (bs, ts, ATT_HEADS, ATT_HEAD_DIM)
            return y_att, k3.reshape(heads), v3.reshape(heads)

        xs, ksn, vsn, cs, sss = _trunk(xs, state_conv[l], state_ssm[l], wts, attend_s, (LANES, ts), bs * ts,
                                       bs * ts, D_FF // 2)
        for lst, val in zip(outs, (kp, vp, ksn, vsn, cp, cs, ssp, sss)):
            lst.append(val)
    return (xp, xs) + tuple(jnp.stack(o) for o in outs)
```

```python
import functools

import jax
import jax.numpy as jnp
from jax import lax
from jax.experimental import pallas as pl
from jax.experimental.pallas import tpu as pltpu

F32 = jnp.float32
BF16 = jnp.bfloat16

D_MODEL = 1024
D_INNER = 2048
SSM_HEADS = 32
SSM_HEAD_DIM = 64
SSM_GROUPS = 4
SSM_STATE = 128
CONV_WIDTH = 4
CONV_DIM = D_INNER + 2 * SSM_GROUPS * SSM_STATE
SSD_CHUNK = 256
ATT_HEADS = 16
ATT_HEAD_DIM = 64
D_ATT = ATT_HEADS * ATT_HEAD_DIM
ROPE_DIM = 16
ROPE_HALF = ROPE_DIM // 2
ROPE_THETA = 500000.0
MOBA_BLOCK = 256
MOBA_TOPK = 3
PAGE_SIZE = 128
PAGES_PER_BLOCK = MOBA_BLOCK // PAGE_SIZE
D_FF = 2816
RMS_EPS = 1e-6
NEG_INF = float("-inf")
MASK_BIAS = -1e30
LOG2_E = 1.4426950408889634

LANES = 128
SUBLANES = 8
V7X_VMEM_BYTES = 64 * 1024 * 1024

COL_XBC = 0
COL_Q = COL_XBC + CONV_DIM
COL_Z = COL_Q + D_ATT
COL_G = COL_Z + D_INNER
COL_K = COL_G + 2 * D_MODEL
PROJ_COLS = COL_K + D_ATT
PROJ_TILE = 1024
DT_PAD = LANES


def _vmem_limit(nbytes):
    return int(min(nbytes, V7X_VMEM_BYTES - 4 * 1024 * 1024))


def _dot(a, b):
    return jnp.dot(a, b, preferred_element_type=F32)


def _dot_nt(a, b):
    return lax.dot_general(a, b, (((1,), (1,)), ((), ())), preferred_element_type=F32)


def _dot_tn(a, b):
    return lax.dot_general(a, b, (((0,), (0,)), ((), ())), preferred_element_type=F32)


def _split2(x):
    hi = x.astype(BF16)
    lo = (x - hi.astype(F32)).astype(BF16)
    return hi, lo


def _split3(x):
    hi = x.astype(BF16)
    r = x - hi.astype(F32)
    mid = r.astype(BF16)
    lo = (r - mid.astype(F32)).astype(BF16)
    return hi, mid, lo


def _dot_f32(a, b):
    a_hi, a_lo = _split2(a)
    b_hi, b_lo = _split2(b)
    return _dot(a_hi, b_hi) + _dot(a_hi, b_lo) + _dot(a_lo, b_hi)


def _sigmoid(x):
    return 1.0 / (1.0 + jnp.exp(-x))


def _softplus(x):
    return jnp.maximum(x, 0.0) + jnp.log1p(jnp.exp(-jnp.abs(x)))


def _in_proj_kernel(x_ref, nw_ref, w_ref, wdt_hi_ref, wdt_lo_ref, proj_ref, v_ref, dt_ref, h_ref, *, n_col):
    j = pl.program_id(1)

    @pl.when(j == 0)
    def _():
        x = x_ref[...]
        ms = jnp.mean(x * x, axis=-1, keepdims=True)
        h = x * lax.rsqrt(ms + RMS_EPS) * nw_ref[...]
        h_hi, h_lo = _split2(h)
        h_ref[...] = h_hi
        dt_ref[...] = (_dot(h_hi, wdt_hi_ref[...]) + _dot(h_hi, wdt_lo_ref[...])
                       + _dot(h_lo, wdt_hi_ref[...]))

    acc = _dot(h_ref[...], w_ref[...])

    @pl.when(j < n_col - 1)
    def _():
        proj_ref[...] = acc

    @pl.when(j == n_col - 1)
    def _():
        v_ref[...] = acc


def _in_proj(x2d, norm_w, w_main, wdt_hi, wdt_lo, tm):
    n = x2d.shape[0]
    n_col = w_main.shape[1] // PROJ_TILE
    n_proj = PROJ_COLS // PROJ_TILE
    vmem = (2 * tm * D_MODEL * 4 + 2 * D_MODEL * PROJ_TILE * 2 + 4 * tm * PROJ_TILE * 4
            + 2 * tm * DT_PAD * 4 + tm * D_MODEL * 2 + 3 * tm * PROJ_TILE * 4 + (4 << 20))
    return pl.pallas_call(
        functools.partial(_in_proj_kernel, n_col=n_col),
        grid=(n // tm, n_col),
        in_specs=[
            pl.BlockSpec((tm, D_MODEL), lambda i, j: (i, 0)),
            pl.BlockSpec((1, D_MODEL), lambda i, j: (0, 0)),
            pl.BlockSpec((D_MODEL, PROJ_TILE), lambda i, j: (0, j)),
            pl.BlockSpec((D_MODEL, DT_PAD), lambda i, j: (0, 0)),
            pl.BlockSpec((D_MODEL, DT_PAD), lambda i, j: (0, 0)),
        ],
        out_specs=[
            pl.BlockSpec((tm, PROJ_TILE), lambda i, j: (i, jnp.minimum(j, n_proj - 1))),
            pl.BlockSpec((tm, PROJ_TILE), lambda i, j: (i, 0)),
            pl.BlockSpec((tm, DT_PAD), lambda i, j: (i, 0)),
        ],
        out_shape=[
            jax.ShapeDtypeStruct((n, PROJ_COLS), F32),
            jax.ShapeDtypeStruct((n, D_ATT), F32),
            jax.ShapeDtypeStruct((n, DT_PAD), F32),
        ],
        scratch_shapes=[pltpu.VMEM((tm, D_MODEL), BF16)],
        compiler_params=pltpu.CompilerParams(
            dimension_semantics=("arbitrary", "arbitrary"), vmem_limit_bytes=_vmem_limit(vmem)),
        name="in_proj",
    )(x2d, norm_w, w_main, wdt_hi, wdt_lo)


def _ssd_kernel(xbc_ref, z_ref, dt_ref, convp_ref, ssmp_ref, convw_ref, convb_ref, dtb_ref, alog_ref,
                dskip_ref, normw_ref, tri_ref, expand_ref, expand_t_ref,
                y_ref, ssm_out_ref, h_ref, ext_ref, ybuf_ref, *, L, T, n_chunks):
    c = pl.program_id(1)
    gn = SSM_GROUPS * SSM_STATE
    hpg = SSM_HEADS // SSM_GROUPS
    gw = hpg * SSM_HEAD_DIM

    @pl.when(c == 0)
    def _():
        h_ref[...] = ssmp_ref[0]
        ext_ref[0:SUBLANES, :] = convp_ref[0]

    ext_ref[SUBLANES:SUBLANES + T, :] = xbc_ref[0]
    if T < L:
        ext_ref[SUBLANES + T:SUBLANES + L, :] = jnp.zeros((L - T, CONV_DIM), F32)

    cw = convw_ref[...]
    conv = convb_ref[...] + cw[3:4, :] * ext_ref[SUBLANES:SUBLANES + L, :]
    conv = conv + cw[2:3, :] * ext_ref[SUBLANES - 1:SUBLANES - 1 + L, :]
    conv = conv + cw[1:2, :] * ext_ref[SUBLANES - 2:SUBLANES - 2 + L, :]
    conv = conv + cw[0:1, :] * ext_ref[SUBLANES - 3:SUBLANES - 3 + L, :]
    ext_ref[0:SUBLANES, :] = ext_ref[L:L + SUBLANES, :]

    xbc = conv * _sigmoid(conv)
    xs = xbc[:, :D_INNER]
    bm = xbc[:, D_INNER:D_INNER + gn]
    cm = xbc[:, D_INNER + gn:]

    row_l = lax.broadcasted_iota(jnp.int32, (L, DT_PAD), 0)
    if T < L:
        dt_raw = jnp.concatenate([dt_ref[0], jnp.zeros((L - T, DT_PAD), F32)], axis=0)
    else:
        dt_raw = dt_ref[0]
    dt = jnp.where(row_l < T, _softplus(dt_raw + dtb_ref[...]), 0.0)
    a = -jnp.exp(alog_ref[...])
    dta = dt * a
    d_hi, d_mid, d_lo = _split3(dta)
    tri = tri_ref[...]
    acum = _dot(tri, d_hi) + _dot(tri, d_mid) + _dot(tri, d_lo)
    acum_t = acum.T
    last = acum[L - 1:L, :]
    e_last = jnp.exp(last)
    stacked = jnp.concatenate([dt, jnp.exp(acum), jnp.exp(last - acum) * dt], axis=0)
    s_hi, s_lo = _split2(stacked)
    ex = expand_ref[...]
    wide = _dot(s_hi, ex) + _dot(s_lo, ex)
    dt_x = wide[0:L]
    ea_x = wide[L:2 * L]
    elm_x = wide[2 * L:3 * L]
    xdt_b = (xs * dt_x).astype(BF16)
    xdec_b = (xs * elm_x).astype(BF16)

    row = lax.broadcasted_iota(jnp.int32, (L, L), 0)
    col = lax.broadcasted_iota(jnp.int32, (L, L), 1)
    causal = row >= col
    lane = lax.broadcasted_iota(jnp.int32, (L, 2 * SSM_HEAD_DIM), 1)
    first_head = lane < SSM_HEAD_DIM

    for g in range(SSM_GROUPS):
        bg = bm[:, g * SSM_STATE:(g + 1) * SSM_STATE].astype(BF16)
        cg = cm[:, g * SSM_STATE:(g + 1) * SSM_STATE].astype(BF16)
        cb = _dot_nt(cg, bg)
        for pr in range(hpg // 2):
            ha = g * hpg + 2 * pr
            c0 = ha * SSM_HEAD_DIM
            xp = xdt_b[:, c0:c0 + 2 * SSM_HEAD_DIM]
            ys = []
            for hh in (ha, ha + 1):
                dec = jnp.exp(jnp.where(causal, acum[:, hh:hh + 1] - acum_t[hh:hh + 1, :], NEG_INF))
                ys.append(_dot((cb * dec).astype(BF16), xp))
            ybuf_ref[:, c0:c0 + 2 * SSM_HEAD_DIM] = jnp.where(first_head, ys[0], ys[1])
        r0 = g * gw
        hg = h_ref[r0:r0 + gw, :]
        y_state = _dot_nt(cg, hg.astype(BF16)) * ea_x[:, r0:r0 + gw]
        ybuf_ref[:, r0:r0 + gw] = ybuf_ref[:, r0:r0 + gw] + y_state
        upd = _dot_tn(xdec_b[:, r0:r0 + gw], bg)
        el_col = jnp.sum(expand_t_ref[r0:r0 + gw, :] * e_last, axis=1, keepdims=True)
        h_ref[r0:r0 + gw, :] = hg * el_col + upd

    y = ybuf_ref[...] + dskip_ref[...] * xs
    y = y[0:T]
    z = z_ref[0]
    y = y * (z * _sigmoid(z))
    for g in range(SSM_GROUPS):
        yg = y[:, g * gw:(g + 1) * gw]
        ms = jnp.mean(yg * yg, axis=-1, keepdims=True)
        y_ref[0, :, g * gw:(g + 1) * gw] = (yg * lax.rsqrt(ms + RMS_EPS) * normw_ref[:, g * gw:(g + 1) * gw]).astype(BF16)

    @pl.when(c == n_chunks - 1)
    def _():
        ssm_out_ref[0] = h_ref[...]


def _ssd(proj3, dt3, conv_prev8, ssm_prev, conv_w8, conv_b, dt_bias, a_log, d_skip_x, norm_w, L, T):
    b, t_total, _ = proj3.shape
    n_chunks = t_total // T
    tri = (jnp.arange(L)[:, None] >= jnp.arange(L)[None, :]).astype(BF16)
    head_of = jnp.arange(D_INNER) // SSM_HEAD_DIM
    expand = (jnp.arange(DT_PAD)[:, None] == head_of[None, :]).astype(BF16)
    expand_t = (head_of[:, None] == jnp.arange(DT_PAD)[None, :]).astype(F32)
    const = lambda bi, ci: (0, 0)
    vmem = (2 * T * CONV_DIM * 4 + 2 * T * D_INNER * 4 + 4 * D_INNER * SSM_STATE * 4 + (L + SUBLANES) * CONV_DIM * 4
            + L * D_INNER * 4 + 2 * T * D_INNER * 2 + 2 * (DT_PAD * D_INNER * 2 + D_INNER * DT_PAD * 4)
            + 14 * L * D_INNER * 4 + (8 << 20))
    return pl.pallas_call(
        functools.partial(_ssd_kernel, L=L, T=T, n_chunks=n_chunks),
        grid=(b, n_chunks),
        in_specs=[
            pl.BlockSpec((1, T, CONV_DIM), lambda bi, ci: (bi, ci, COL_XBC // CONV_DIM)),
            pl.BlockSpec((1, T, D_INNER), lambda bi, ci: (bi, ci, COL_Z // D_INNER)),
            pl.BlockSpec((1, T, DT_PAD), lambda bi, ci: (bi, ci, 0)),
            pl.BlockSpec((1, SUBLANES, CONV_DIM), lambda bi, ci: (bi, 0, 0)),
            pl.BlockSpec((1, D_INNER, SSM_STATE), lambda bi, ci: (bi, 0, 0)),
            pl.BlockSpec((SUBLANES, CONV_DIM), const),
            pl.BlockSpec((1, CONV_DIM), const),
            pl.BlockSpec((1, DT_PAD), const),
            pl.BlockSpec((1, DT_PAD), const),
            pl.BlockSpec((1, D_INNER), const),
            pl.BlockSpec((1, D_INNER), const),
            pl.BlockSpec((L, L), const),
            pl.BlockSpec((DT_PAD, D_INNER), const),
            pl.BlockSpec((D_INNER, DT_PAD), const),
        ],
        out_specs=[
            pl.BlockSpec((1, T, D_INNER), lambda bi, ci: (bi, ci, 0)),
            pl.BlockSpec((1, D_INNER, SSM_STATE), lambda bi, ci: (bi, 0, 0)),
        ],
        out_shape=[
            jax.ShapeDtypeStruct((b, t_total, D_INNER), BF16),
            jax.ShapeDtypeStruct((b, D_INNER, SSM_STATE), F32),
        ],
        scratch_shapes=[
            pltpu.VMEM((D_INNER, SSM_STATE), F32),
            pltpu.VMEM((L + SUBLANES, CONV_DIM), F32),
            pltpu.VMEM((L, D_INNER), F32),
        ],
        compiler_params=pltpu.CompilerParams(
            dimension_semantics=("arbitrary", "arbitrary"), vmem_limit_bytes=_vmem_limit(vmem)),
        name="ssd",
    )(proj3, proj3, dt3, conv_prev8, ssm_prev, conv_w8, conv_b, dt_bias, a_log, d_skip_x, norm_w,
      tri, expand, expand_t)


FLASH_CHUNK = 4


def _norm_rope_t(x_t, w_col, cos, sin):
    outs = []
    for h in range(2):
        xh = x_t[h * ATT_HEAD_DIM:(h + 1) * ATT_HEAD_DIM, :]
        ms = jnp.mean(xh * xh, axis=0, keepdims=True)
        xn = xh * lax.rsqrt(ms + RMS_EPS) * w_col[h * ATT_HEAD_DIM:(h + 1) * ATT_HEAD_DIM, :]
        x1 = xn[0:ROPE_HALF, :]
        x2 = xn[ROPE_HALF:ROPE_DIM, :]
        outs += [x1 * cos - x2 * sin, x2 * cos + x1 * sin, xn[ROPE_DIM:, :]]
    return jnp.concatenate(outs, axis=0)


def _moba_prompt_kernel(q_ref, k_ref, v_ref, qw_ref, kw_ref, cos_ref, sin_ref, y_ref, kout_ref, vout_ref,
                        qt_ref, qtf_ref, kn_ref, vtc_ref, vtb_ref, km_ref, s_ref, mx_ref, m_ref, l_ref, acc_ref,
                        *, n_blocks):
    blk = MOBA_BLOCK
    cw = FLASH_CHUNK
    n_chunks = n_blocks // cw
    hd = ATT_HEAD_DIM
    pair_w = 2 * hd
    scale = (ATT_HEAD_DIM ** -0.5) * LOG2_E
    lane_k = lax.broadcasted_iota(jnp.int32, (blk, pair_w), 1)

    for i in range(n_blocks):
        r0 = i * blk
        c0 = (i % cw) * blk
        cos = cos_ref[i]
        sin = sin_ref[i]
        qn = _norm_rope_t(q_ref[0, r0:r0 + blk, :].T, qw_ref[...], cos, sin)
        qtf_ref[i] = qn
        qs = (qn * scale).astype(BF16)
        qt_ref[0, i, 0:hd, :] = qs[0:hd, :]
        qt_ref[1, i, 0:hd, :] = qs[hd:pair_w, :]
        kn_t = _norm_rope_t(k_ref[0, r0:r0 + blk, :].T, kw_ref[...], cos, sin)
        kout_ref[0, :, r0:r0 + blk] = kn_t
        kn = kn_t.T
        km_ref[i:i + 1, :] = jnp.mean(kn, axis=0, keepdims=True)
        onehot = (lane_k == hd + i).astype(F32)
        kn_ref[0, i // cw, c0:c0 + blk, :] = jnp.where(lane_k < hd, kn, onehot).astype(BF16)
        kn_ref[1, i // cw, c0:c0 + blk, :] = jnp.where(lane_k < hd, pltpu.roll(kn, hd, 1), onehot).astype(BF16)
        vt = v_ref[0, r0:r0 + blk, :].T
        vout_ref[0, :, r0:r0 + blk] = vt
        vt = vt.astype(BF16)
        vtb_ref[i] = vt
        vtc_ref[i // cw, :, c0:c0 + blk] = vt

    km = km_ref[...]
    km_lane = lax.broadcasted_iota(jnp.int32, km.shape, 1)
    km_heads = (jnp.where(km_lane < hd, km, 0.0), jnp.where(km_lane < hd, 0.0, km))
    jrow = lax.broadcasted_iota(jnp.int32, (n_blocks, blk), 0)
    k_sel = min(MOBA_TOPK, n_blocks)
    pad_rows = jnp.zeros((hd - n_blocks, blk), BF16)

    for i in range(n_blocks):
        qtf = qtf_ref[i]
        for h in range(2):
            gate = _dot_f32(km_heads[h], qtf)
            valid = jrow < i
            gate = jnp.where(valid, gate, NEG_INF)
            cnt = jnp.zeros((n_blocks, blk), jnp.int32)
            for jp in range(i):
                gj = gate[jp:jp + 1, :]
                better = (gj > gate) | ((gj == gate) & (jp < jrow))
                cnt = cnt + better.astype(jnp.int32)
            sel = valid & (cnt < k_sel)
            bias = jnp.where(sel, 0.0, MASK_BIAS).astype(BF16)
            qt_ref[h, i, hd:pair_w, :] = jnp.concatenate([bias, pad_rows], axis=0)

    key_i = lax.broadcasted_iota(jnp.int32, (blk, blk), 0)
    qry_i = lax.broadcasted_iota(jnp.int32, (blk, blk), 1)
    no_bias = jnp.zeros((hd, blk), BF16)
    mx_ref[...] = jnp.full(mx_ref.shape, NEG_INF, F32)

    def score_tile(c, buf, h, u, qt):
        t = _dot(kn_ref[h, c, u * blk:(u + 1) * blk, :], qt)
        s_ref[buf, h, u * blk:(u + 1) * blk, :] = t
        mx_ref[buf, h, u:u + 1, :] = jnp.max(t, axis=0, keepdims=True)

    def chunk_scores(c, buf, h, qt):
        for u in range(cw):
            score_tile(c, buf, h, u, qt)

    def attend(i, carry):
        r0 = pl.multiple_of(i * blk, blk)
        ic = lax.div(i, cw)
        c0 = pl.multiple_of(lax.rem(i, cw) * blk, blk)
        vd = vtb_ref[i]
        qts = [qt_ref[h, i] for h in range(2)]
        s_own = [_dot(kn_ref[h, ic, pl.ds(c0, blk), :], jnp.concatenate([qts[h][0:hd, :], no_bias], axis=0))
                 for h in range(2)]
        ps = []
        for h in range(2):
            s = jnp.where(key_i <= qry_i, s_own[h], NEG_INF)
            m = jnp.max(s, axis=0, keepdims=True)
            p = jnp.exp2(s - m)
            m_ref[h] = m
            l_ref[h] = jnp.sum(p, axis=0, keepdims=True)
            ps.append(p.astype(BF16))
        for h in range(2):
            chunk_scores(0, 0, h, qts[h])
        for h in range(2):
            acc_ref[h] = _dot(vd[h * hd:(h + 1) * hd, :], ps[h])

        for c in range(n_chunks):
            @pl.when(c * cw < i)
            def _(c=c):
                slot = c % 2
                m1s, alphas, l1s, pvs = [], [], [], [None, None]
                for h in range(2):
                    m0 = m_ref[h]
                    m1 = m0
                    for u in range(cw):
                        m1 = jnp.maximum(m1, mx_ref[slot, h, u:u + 1, :])
                    m1s.append(m1)
                    alphas.append(jnp.exp2(m0 - m1))
                    l1s.append(alphas[h] * l_ref[h])
                for u in range(cw):
                    for h in range(2):
                        if c + 1 < n_chunks:
                            score_tile(c + 1, 1 - slot, h, u, qts[h])
                        p = jnp.exp2(s_ref[slot, h, u * blk:(u + 1) * blk, :] - m1s[h])
                        l1s[h] = l1s[h] + jnp.sum(p, axis=0, keepdims=True)
                        d = _dot(vtc_ref[c, h * hd:(h + 1) * hd, u * blk:(u + 1) * blk], p.astype(BF16))
                        pvs[h] = d if pvs[h] is None else pvs[h] + d
                for h in range(2):
                    m_ref[h] = m1s[h]
                    l_ref[h] = l1s[h]
                    acc_ref[h] = alphas[h] * acc_ref[h] + pvs[h]

        o_t = jnp.concatenate([acc_ref[0] / l_ref[0], acc_ref[1] / l_ref[1]], axis=0)
        y_ref[0, pl.ds(r0, blk), :] = o_t.T.astype(BF16)
        return carry

    lax.fori_loop(0, n_blocks, attend, 0)


def _moba_prompt(proj3, v3, q_norm_w, k_norm_w, cos_t, sin_t):
    b, s, _ = proj3.shape
    n_blocks = s // MOBA_BLOCK
    assert n_blocks % FLASH_CHUNK == 0 and n_blocks <= ATT_HEAD_DIM
    n_chunks = n_blocks // FLASH_CHUNK
    n_pairs = ATT_HEADS // 2
    pw = 2 * ATT_HEAD_DIM
    cblk = FLASH_CHUNK * MOBA_BLOCK
    qw = jnp.tile(q_norm_w, 2).reshape(pw, 1)
    kw = jnp.tile(k_norm_w, 2).reshape(pw, 1)
    const3 = lambda bi, p: (0, 0, 0)
    vmem = (2 * 3 * s * pw * 4 + 2 * s * pw * 2 + 4 * s * pw * 4 + 7 * s * pw * 2 + s * pw * 4
            + 4 * cblk * MOBA_BLOCK * 4 + (16 << 20))
    y_att, k_t, v_t = pl.pallas_call(
        functools.partial(_moba_prompt_kernel, n_blocks=n_blocks),
        grid=(b, n_pairs),
        in_specs=[
            pl.BlockSpec((1, s, pw), lambda bi, p: (bi, 0, COL_Q // pw + p)),
            pl.BlockSpec((1, s, pw), lambda bi, p: (bi, 0, COL_K // pw + p)),
            pl.BlockSpec((1, s, pw), lambda bi, p: (bi, 0, p)),
            pl.BlockSpec((pw, 1), lambda bi, p: (0, 0)),
            pl.BlockSpec((pw, 1), lambda bi, p: (0, 0)),
            pl.BlockSpec((n_blocks, ROPE_HALF, MOBA_BLOCK), const3),
            pl.BlockSpec((n_blocks, ROPE_HALF, MOBA_BLOCK), const3),
        ],
        out_specs=[
            pl.BlockSpec((1, s, pw), lambda bi, p: (bi, 0, p)),
            pl.BlockSpec((1, pw, s), lambda bi, p: (bi, p, 0)),
            pl.BlockSpec((1, pw, s), lambda bi, p: (bi, p, 0)),
        ],
        out_shape=[
            jax.ShapeDtypeStruct((b, s, D_ATT), BF16),
            jax.ShapeDtypeStruct((b, D_ATT, s), F32),
            jax.ShapeDtypeStruct((b, D_ATT, s), F32),
        ],
        scratch_shapes=[
            pltpu.VMEM((2, n_blocks, pw, MOBA_BLOCK), BF16),
            pltpu.VMEM((n_blocks, pw, MOBA_BLOCK), F32),
            pltpu.VMEM((2, n_chunks, cblk, pw), BF16),
            pltpu.VMEM((n_chunks, pw, cblk), BF16),
            pltpu.VMEM((n_blocks, pw, MOBA_BLOCK), BF16),
            pltpu.VMEM((n_blocks, pw), F32),
            pltpu.VMEM((2, 2, cblk, MOBA_BLOCK), F32),
            pltpu.VMEM((2, 2, SUBLANES, MOBA_BLOCK), F32),
            pltpu.VMEM((2, 1, MOBA_BLOCK), F32),
            pltpu.VMEM((2, 1, MOBA_BLOCK), F32),
            pltpu.VMEM((2, ATT_HEAD_DIM, MOBA_BLOCK), F32),
        ],
        compiler_params=pltpu.CompilerParams(
            dimension_semantics=("arbitrary", "arbitrary"), vmem_limit_bytes=_vmem_limit(vmem)),
        name="moba_prompt",
    )(proj3, proj3, v3, qw, kw, cos_t, sin_t)
    to_heads = lambda t: t.reshape(b, ATT_HEADS, ATT_HEAD_DIM, s).transpose(0, 3, 1, 2)
    return y_att, to_heads(k_t), to_heads(v_t)


def _seg_sum(x, seg):
    w = x.shape[-1]
    lane = lax.broadcasted_iota(jnp.int32, x.shape, x.ndim - 1)
    s = 1
    while s < seg:
        x = x + jnp.where((lane & s) != 0, pltpu.roll(x, s, x.ndim - 1), pltpu.roll(x, w - s, x.ndim - 1))
        s *= 2
    return x


def _norm_rope_rows(x, w_row, cosf, sin_lo, sin_hi):
    ms = _seg_sum(x * x, ATT_HEAD_DIM) * (1.0 / ATT_HEAD_DIM)
    xn = x * lax.rsqrt(ms + RMS_EPS) * w_row
    width = x.shape[-1]
    return (xn * cosf + pltpu.roll(xn, width - ROPE_HALF, 1) * sin_lo + pltpu.roll(xn, ROPE_HALF, 1) * sin_hi)


def _moba_sample_kernel(pt_ref, q_ref, k_ref, v_ref, qw_ref, kw_ref, cosf_ref, slo_ref, shi_ref, hmask_ref,
                        *rest, T, pages_per_step, n_steps, n_blocks):
    kp = rest[:pages_per_step]
    vp = rest[pages_per_step:2 * pages_per_step]
    y_ref, kout_ref = rest[2 * pages_per_step:2 * pages_per_step + 2]
    qbd_ref, knp_ref, vnp_ref, ms_ref, ls_ref, gs_ref, o_ref = rest[2 * pages_per_step + 2:]
    step = pl.program_id(1)
    rows = ATT_HEADS * T
    scale = ATT_HEAD_DIM ** -0.5
    hmask = hmask_ref[...]

    def fold(o_all):
        o_all = o_all * hmask
        acc = o_all[:, 0:LANES]
        for u in range(1, D_ATT // LANES):
            acc = acc + o_all[:, u * LANES:(u + 1) * LANES]
        return acc

    @pl.when(step == 0)
    def _():
        qn = _norm_rope_rows(q_ref[0], qw_ref[...], cosf_ref[...], slo_ref[...], shi_ref[...])
        kn = _norm_rope_rows(k_ref[0], kw_ref[...], cosf_ref[...], slo_ref[...], shi_ref[...])
        kout_ref[0] = kn
        q_rows = jnp.concatenate([qn * scale] * ATT_HEADS, axis=0) * hmask
        q_hi, q_lo = _split2(q_rows)
        qbd_ref[0:rows, :] = q_hi
        qbd_ref[rows:2 * rows, :] = q_lo
        pad = jnp.zeros((rows - T, D_ATT), F32)
        knp_ref[...] = jnp.concatenate([kn, pad], axis=0).astype(BF16)
        vnp_ref[...] = jnp.concatenate([v_ref[0], pad], axis=0).astype(BF16)
        ms_ref[...] = jnp.full((rows, LANES), NEG_INF, F32)
        gs_ref[...] = jnp.full((rows, LANES), NEG_INF, F32)
        ls_ref[...] = jnp.zeros((rows, LANES), F32)

    lane = lax.broadcasted_iota(jnp.int32, (rows, LANES), 1)
    qbd = qbd_ref[...]
    blocks_per_step = pages_per_step // PAGES_PER_BLOCK
    for jj in range(blocks_per_step):
        j = step * blocks_per_step + jj
        kt = jnp.concatenate([kp[jj * PAGES_PER_BLOCK + u][0, 0].reshape(D_ATT, PAGE_SIZE)
                              for u in range(PAGES_PER_BLOCK)], axis=1).astype(BF16)
        vt = jnp.concatenate([vp[jj * PAGES_PER_BLOCK + u][0, 0].reshape(D_ATT, PAGE_SIZE)
                              for u in range(PAGES_PER_BLOCK)], axis=1).astype(BF16)
        s2 = _dot(qbd, kt)
        s = s2[0:rows, :] + s2[rows:2 * rows, :]
        g_j = jnp.mean(s, axis=1, keepdims=True)
        m_j = jnp.max(s, axis=1, keepdims=True)
        p = jnp.exp(s - m_j)
        l_j = jnp.sum(p, axis=1, keepdims=True)
        o_ref[j] = fold(_dot_nt(p.astype(BF16), vt))
        here = lane == j
        ms_ref[...] = jnp.where(here, m_j, ms_ref[...])
        ls_ref[...] = jnp.where(here, l_j, ls_ref[...])
        gs_ref[...] = jnp.where(here, g_j, gs_ref[...])

    @pl.when(step == n_steps - 1)
    def _():
        r = lax.broadcasted_iota(jnp.int32, (rows, LANES), 0)
        s2 = _dot_nt(qbd, knp_ref[...])
        s_own = s2[0:rows, :] + s2[rows:2 * rows, :]
        s_own = jnp.where((lane < T) & (lane <= (r % T)), s_own, NEG_INF)
        m_o = jnp.max(s_own, axis=1, keepdims=True)
        p_o = jnp.exp(s_own - m_o)
        l_o = jnp.sum(p_o, axis=1, keepdims=True)
        o_o = fold(_dot(p_o.astype(BF16), vnp_ref[...]))

        gs = gs_ref[...]
        cnt = jnp.zeros((rows, LANES), jnp.int32)
        for jp in range(n_blocks):
            gj = gs[:, jp:jp + 1]
            better = (gj > gs) | ((gj == gs) & (jp < lane))
            cnt = cnt + better.astype(jnp.int32)
        sel = (lane < n_blocks) & (cnt < min(MOBA_TOPK, n_blocks))
        ms = ms_ref[...]
        m_tot = jnp.maximum(jnp.max(jnp.where(sel, ms, NEG_INF), axis=1, keepdims=True), m_o)
        w = jnp.where(sel, jnp.exp(ms - m_tot), 0.0)
        w_o = jnp.exp(m_o - m_tot)
        l_tot = jnp.sum(w * ls_ref[...], axis=1, keepdims=True) + w_o * l_o
        acc = w_o * o_o
        for jp in range(n_blocks):
            acc = acc + w[:, jp:jp + 1] * o_ref[jp]
        out = acc / l_tot
        even_head = lax.broadcasted_iota(jnp.int32, (T, LANES), 1) < ATT_HEAD_DIM
        y_ref[0] = jnp.concatenate(
            [jnp.where(even_head, out[2 * hp * T:(2 * hp + 1) * T, :], out[(2 * hp + 1) * T:(2 * hp + 2) * T, :])
             for hp in range(ATT_HEADS // 2)], axis=1).astype(BF16)


def _moba_sample(q3, k3, v3, cache_k, cache_v, page_table, layer, q_norm_w, k_norm_w, cosf, sin_lo, sin_hi):
    nseq, T, _ = v3.shape
    n_pages = page_table.shape[1]
    assert T == SUBLANES and n_pages % PAGES_PER_BLOCK == 0
    n_blocks = n_pages // PAGES_PER_BLOCK
    assert n_blocks <= LANES
    pages_per_step = 16 if n_pages % 16 == 0 else PAGES_PER_BLOCK
    assert n_pages % pages_per_step == 0
    n_steps = n_pages // pages_per_step
    rows = ATT_HEADS * T
    hmask = ((jnp.arange(rows)[:, None] // T) == (jnp.arange(D_ATT)[None, :] // ATT_HEAD_DIM)).astype(F32)
    qw = jnp.tile(q_norm_w, ATT_HEADS).reshape(1, D_ATT)
    kw = jnp.tile(k_norm_w, ATT_HEADS).reshape(1, D_ATT)
    ck_t = jnp.transpose(cache_k, (0, 1, 3, 4, 2))
    cv_t = jnp.transpose(cache_v, (0, 1, 3, 4, 2))

    tok = lambda b, s, pt: (b, 0, 0)
    const = lambda b, s, pt: (0, 0)
    page_shape = (1, 1, ATT_HEADS, ATT_HEAD_DIM, PAGE_SIZE)

    def page_spec(u):
        return pl.BlockSpec(page_shape, lambda b, s, pt, u=u: (layer, pt[b, s * pages_per_step + u], 0, 0, 0))

    in_specs = [
        pl.BlockSpec((1, T, D_ATT), lambda b, s, pt: (b, 0, COL_Q // D_ATT)),
        pl.BlockSpec((1, T, D_ATT), lambda b, s, pt: (b, 0, COL_K // D_ATT)),
        pl.BlockSpec((1, T, D_ATT), tok),
        pl.BlockSpec((1, D_ATT), const),
        pl.BlockSpec((1, D_ATT), const),
        pl.BlockSpec((T, D_ATT), const),
        pl.BlockSpec((T, D_ATT), const),
        pl.BlockSpec((T, D_ATT), const),
        pl.BlockSpec((rows, D_ATT), const),
    ] + [page_spec(u) for u in range(pages_per_step)] * 2
    page_vmem = PAGE_SIZE * D_ATT * 4
    vmem = 2 * 2 * pages_per_step * page_vmem + n_blocks * rows * LANES * 4 + (24 << 20)
    return pl.pallas_call(
        functools.partial(_moba_sample_kernel, T=T, pages_per_step=pages_per_step, n_steps=n_steps,
                          n_blocks=n_blocks),
        grid_spec=pltpu.PrefetchScalarGridSpec(
            num_scalar_prefetch=1,
            grid=(nseq, n_steps),
            in_specs=in_specs,
            out_specs=[pl.BlockSpec((1, T, D_ATT), tok), pl.BlockSpec((1, T, D_ATT), tok)],
            scratch_shapes=[
                pltpu.VMEM((2 * rows, D_ATT), BF16),
                pltpu.VMEM((rows, D_ATT), BF16),
                pltpu.VMEM((rows, D_ATT), BF16),
                pltpu.VMEM((rows, LANES), F32),
                pltpu.VMEM((rows, LANES), F32),
                pltpu.VMEM((rows, LANES), F32),
                pltpu.VMEM((n_blocks, rows, LANES), F32),
            ],
        ),
        out_shape=[
            jax.ShapeDtypeStruct((nseq, T, D_ATT), BF16),
            jax.ShapeDtypeStruct((nseq, T, D_ATT), F32),
        ],
        compiler_params=pltpu.CompilerParams(
            dimension_semantics=("arbitrary", "arbitrary"), vmem_limit_bytes=_vmem_limit(vmem)),
        name="moba_sample",
    )(page_table, q3, k3, v3, qw, kw, cosf, sin_lo, sin_hi, hmask,
      *([ck_t] * pages_per_step), *([cv_t] * pages_per_step))


def _merge_kernel(x_ref, yssm_ref, yatt_ref, g_ref, wssm_ref, watt_ref, wout_ref, n2w_ref, x1_ref, h2_ref):
    u_ssm = _dot(yssm_ref[...], wssm_ref[...])
    u_att = _dot(yatt_ref[...], watt_ref[...])
    g = _sigmoid(g_ref[...])
    merged = (g[:, :D_MODEL] * u_ssm + g[:, D_MODEL:] * u_att).astype(BF16)
    x1 = x_ref[...] + _dot(merged, wout_ref[...])
    x1_ref[...] = x1
    ms = jnp.mean(x1 * x1, axis=-1, keepdims=True)
    h2_ref[...] = (x1 * lax.rsqrt(ms + RMS_EPS) * n2w_ref[...]).astype(BF16)


def _merge(x2d, y_ssm, y_att, proj, w_ssm, w_att, w_out, norm2_w, tm):
    n = x2d.shape[0]
    const = lambda i: (0, 0)
    vmem = (2 * tm * (D_MODEL * 4 + D_INNER * 2 + D_ATT * 2 + 2 * D_MODEL * 4 + D_MODEL * 4 + D_MODEL * 2)
            + 2 * 2 * (D_INNER + D_ATT + D_MODEL) * D_MODEL + 8 * tm * D_MODEL * 4 + (4 << 20))
    return pl.pallas_call(
        _merge_kernel,
        grid=(n // tm,),
        in_specs=[
            pl.BlockSpec((tm, D_MODEL), lambda i: (i, 0)),
            pl.BlockSpec((tm, D_INNER), lambda i: (i, 0)),
            pl.BlockSpec((tm, D_ATT), lambda i: (i, 0)),
            pl.BlockSpec((tm, 2 * D_MODEL), lambda i: (i, COL_G // (2 * D_MODEL))),
            pl.BlockSpec((D_INNER, D_MODEL), const),
            pl.BlockSpec((D_ATT, D_MODEL), const),
            pl.BlockSpec((D_MODEL, D_MODEL), const),
            pl.BlockSpec((1, D_MODEL), const),
        ],
        out_specs=[pl.BlockSpec((tm, D_MODEL), lambda i: (i, 0)), pl.BlockSpec((tm, D_MODEL), lambda i: (i, 0))],
        out_shape=[jax.ShapeDtypeStruct((n, D_MODEL), F32), jax.ShapeDtypeStruct((n, D_MODEL), BF16)],
        compiler_params=pltpu.CompilerParams(
            dimension_semantics=("arbitrary",), vmem_limit_bytes=_vmem_limit(vmem)),
        name="merge",
    )(x2d, y_ssm, y_att, proj, w_ssm, w_att, w_out, norm2_w)


def _ffn_kernel(h2_ref, x1_ref, wg_ref, wu_ref, wd_ref, out_ref):
    f = pl.program_id(1)
    h2 = h2_ref[...]
    gate = _dot(h2, wg_ref[...])
    up = _dot(h2, wu_ref[...])
    act = (gate * _sigmoid(gate) * up).astype(BF16)
    part = _dot(act, wd_ref[...])

    @pl.when(f == 0)
    def _():
        out_ref[...] = x1_ref[...] + part

    @pl.when(f != 0)
    def _():
        out_ref[...] = out_ref[...] + part


def _ffn(h2, x1, w_gate, w_up, w_down, tm, tf):
    n = h2.shape[0]
    vmem = (2 * tm * D_MODEL * (2 + 4 + 4) + 2 * 3 * D_MODEL * tf * 2 + 6 * tm * tf * 4 + (4 << 20))
    return pl.pallas_call(
        _ffn_kernel,
        grid=(n // tm, D_FF // tf),
        in_specs=[
            pl.BlockSpec((tm, D_MODEL), lambda i, f: (i, 0)),
            pl.BlockSpec((tm, D_MODEL), lambda i, f: (i, 0)),
            pl.BlockSpec((D_MODEL, tf), lambda i, f: (0, f)),
            pl.BlockSpec((D_MODEL, tf), lambda i, f: (0, f)),
            pl.BlockSpec((tf, D_MODEL), lambda i, f: (f, 0)),
        ],
        out_specs=pl.BlockSpec((tm, D_MODEL), lambda i, f: (i, 0)),
        out_shape=jax.ShapeDtypeStruct((n, D_MODEL), F32),
        compiler_params=pltpu.CompilerParams(
            dimension_semantics=("arbitrary", "arbitrary"), vmem_limit_bytes=_vmem_limit(vmem)),
        name="ffn",
    )(h2, x1, w_gate, w_up, w_down)


def _rope_angles(pos):
    inv_freq = ROPE_THETA ** (-(jnp.arange(ROPE_HALF, dtype=F32) * 2.0 / ROPE_DIM))
    ang = pos.astype(F32)[:, None] * inv_freq[None, :]
    return jnp.cos(ang), jnp.sin(ang)


def _rope_tables_t(pos, n_blocks):
    cos, sin = _rope_angles(pos)
    to_blocks = lambda t: t.T.reshape(ROPE_HALF, n_blocks, MOBA_BLOCK).transpose(1, 0, 2)
    return to_blocks(cos), to_blocks(sin)


def _rope_tables_rows(pos):
    cos, sin = _rope_angles(pos)
    d = jnp.arange(D_ATT) % ATT_HEAD_DIM
    idx = d % ROPE_HALF
    cosf = jnp.where(d[None, :] < ROPE_DIM, cos[:, idx], 1.0)
    sin_lo = jnp.where(d[None, :] < ROPE_HALF, -sin[:, idx], 0.0)
    sin_hi = jnp.where((d[None, :] >= ROPE_HALF) & (d[None, :] < ROPE_DIM), sin[:, idx], 0.0)
    return cosf.astype(F32), sin_lo.astype(F32), sin_hi.astype(F32)


def _layer_weights(l, norm1_w, w_in, conv_w, conv_b, dt_bias, a_log, d_skip, ssm_norm_w, q_norm_w, k_norm_w,
                   w_ssm_branch, w_att_branch, w_out, norm2_w, w_ffn_in, w_ffn_out):
    w = w_in[l]
    off_xbc = D_INNER
    off_dt = off_xbc + CONV_DIM
    off_q = off_dt + SSM_HEADS
    off_k = off_q + D_ATT
    off_v = off_k + D_ATT
    off_g = off_v + D_ATT
    w_main = jnp.concatenate([w[:, off_xbc:off_dt], w[:, off_q:off_k], w[:, 0:off_xbc], w[:, off_g:],
                              w[:, off_k:off_v], w[:, off_v:off_g]], axis=1).astype(BF16)
    w_dt = jnp.pad(w[:, off_dt:off_q], ((0, 0), (0, DT_PAD - SSM_HEADS)))
    wdt_hi = w_dt.astype(BF16)
    wdt_lo = (w_dt - wdt_hi.astype(F32)).astype(BF16)
    pad_h = (0, DT_PAD - SSM_HEADS)
    return dict(
        norm1_w=norm1_w[l].reshape(1, D_MODEL), w_main=w_main, wdt_hi=wdt_hi, wdt_lo=wdt_lo,
        conv_w8=jnp.pad(conv_w[l], ((0, SUBLANES - CONV_WIDTH), (0, 0))), conv_b=conv_b[l].reshape(1, CONV_DIM),
        dt_bias=jnp.pad(dt_bias[l], pad_h).reshape(1, DT_PAD), a_log=jnp.pad(a_log[l], pad_h).reshape(1, DT_PAD),
        d_skip_x=jnp.repeat(d_skip[l], SSM_HEAD_DIM).reshape(1, D_INNER),
        ssm_norm_w=ssm_norm_w[l].reshape(1, D_INNER), q_norm_w=q_norm_w[l], k_norm_w=k_norm_w[l],
        w_ssm=w_ssm_branch[l].astype(BF16), w_att=w_att_branch[l].astype(BF16), w_out=w_out[l].astype(BF16),
        norm2_w=norm2_w[l].reshape(1, D_MODEL),
        w_gate=w_ffn_in[l][:, :D_FF].astype(BF16), w_up=w_ffn_in[l][:, D_FF:].astype(BF16),
        w_down=w_ffn_out[l].astype(BF16),
    )


def _trunk(x, conv_prev, ssm_prev, wts, attend, ssd_chunk, tm_proj, tm, tf):
    b, t, _ = x.shape
    n = b * t
    x2d = x.reshape(n, D_MODEL)
    proj, v, dt = _in_proj(x2d, wts["norm1_w"], wts["w_main"], wts["wdt_hi"], wts["wdt_lo"], tm_proj)
    proj3 = proj.reshape(b, t, PROJ_COLS)
    v3 = v.reshape(b, t, D_ATT)
    conv_prev8 = jnp.pad(conv_prev, ((0, 0), (SUBLANES - (CONV_WIDTH - 1), 0), (0, 0)))
    L, T = ssd_chunk
    y_ssm, ssm_new = _ssd(proj3, dt.reshape(b, t, DT_PAD), conv_prev8, ssm_prev.reshape(b, D_INNER, SSM_STATE),
                          wts["conv_w8"], wts["conv_b"], wts["dt_bias"], wts["a_log"], wts["d_skip_x"],
                          wts["ssm_norm_w"], L, T)
    y_att, k_heads, v_heads = attend(proj3, v3)
    x1, h2 = _merge(x2d, y_ssm.reshape(n, D_INNER), y_att.reshape(n, D_ATT), proj, wts["w_ssm"], wts["w_att"],
                    wts["w_out"], wts["norm2_w"], tm)
    out = _ffn(h2, x1, wts["w_gate"], wts["w_up"], wts["w_down"], tm, tf)
    conv_new = proj3[:, t - (CONV_WIDTH - 1):, COL_XBC:COL_XBC + CONV_DIM]
    return (out.reshape(b, t, D_MODEL), k_heads, v_heads, conv_new,
            ssm_new.reshape(b, SSM_HEADS, SSM_HEAD_DIM, SSM_STATE))


def kernel(x_prompt, x_sample, cache_k, cache_v, page_table, state_conv, state_ssm, norm1_w, w_in, conv_w, conv_b, dt_bias, a_log, d_skip, ssm_norm_w, q_norm_w, k_norm_w, w_ssm_branch, w_att_branch, w_out, norm2_w, w_ffn_in, w_ffn_out):
    depth = w_in.shape[0]
    bp, sp, _ = x_prompt.shape
    bs, ts, _ = x_sample.shape
    past_len = page_table.shape[1] * PAGE_SIZE
    assert sp % MOBA_BLOCK == 0 and sp % SSD_CHUNK == 0 and ts <= SSD_CHUNK
    cos_t, sin_t = _rope_tables_t(jnp.arange(sp), sp // MOBA_BLOCK)
    cosf, sin_lo, sin_hi = _rope_tables_rows(past_len + jnp.arange(ts))
    xp, xs = x_prompt, x_sample
    outs = [[] for _ in range(8)]
    for l in range(depth):
        wts = _layer_weights(l, norm1_w, w_in, conv_w, conv_b, dt_bias, a_log, d_skip, ssm_norm_w, q_norm_w,
                             k_norm_w, w_ssm_branch, w_att_branch, w_out, norm2_w, w_ffn_in, w_ffn_out)
        attend_p = lambda proj3, v3: _moba_prompt(proj3, v3, wts["q_norm_w"], wts["k_norm_w"], cos_t, sin_t)
        conv0 = jnp.zeros((bp, CONV_WIDTH - 1, CONV_DIM), F32)
        ssm0 = jnp.zeros((bp, SSM_HEADS, SSM_HEAD_DIM, SSM_STATE), F32)
        xp, kp, vp, cp, ssp = _trunk(xp, conv0, ssm0, wts, attend_p, (SSD_CHUNK, SSD_CHUNK), min(1024, bp * sp),
                                     512, D_FF // 2)

        def attend_s(proj3, v3):
            y_att, k3 = _moba_sample(proj3, proj3, v3, cache_k, cache_v, page_table, l, wts["q_norm_w"],
                                     wts["k_norm_w"], cosf, sin_lo, sin_hi)
            heads = (bs, ts, ATT_HEADS, ATT_HEAD_DIM)
            return y_att, k3.reshape(heads), v3.reshape(heads)

        xs, ksn, vsn, cs, sss = _trunk(xs, state_conv[l], state_ssm[l], wts, attend_s, (LANES, ts), bs * ts,
                                       bs * ts, D_FF // 2)
        for lst, val in zip(outs, (kp, vp, ksn, vsn, cp, cs, ssp, sss)):
            lst.append(val)
    return (xp, xs) + tuple(jnp.stack(o) for o in outs)
```

```python
import functools

import jax
import jax.numpy as jnp
from jax import lax
from jax.experimental import pallas as pl
from jax.experimental.pallas import tpu as pltpu

F32 = jnp.float32
BF16 = jnp.bfloat16

D_MODEL = 1024
D_INNER = 2048
SSM_HEADS = 32
SSM_HEAD_DIM = 64
SSM_GROUPS = 4
SSM_STATE = 128
CONV_WIDTH = 4
CONV_DIM = D_INNER + 2 * SSM_GROUPS * SSM_STATE
SSD_CHUNK = 256
ATT_HEADS = 16
ATT_HEAD_DIM = 64
D_ATT = ATT_HEADS * ATT_HEAD_DIM
ROPE_DIM = 16
ROPE_HALF = ROPE_DIM // 2
ROPE_THETA = 500000.0
MOBA_BLOCK = 256
MOBA_TOPK = 3
PAGE_SIZE = 128
PAGES_PER_BLOCK = MOBA_BLOCK // PAGE_SIZE
D_FF = 2816
RMS_EPS = 1e-6
NEG_INF = float("-inf")
MASK_BIAS = -1e30
LOG2_E = 1.4426950408889634

LANES = 128
SUBLANES = 8
V7X_VMEM_BYTES = 64 * 1024 * 1024

COL_XBC = 0
COL_Q = COL_XBC + CONV_DIM
COL_Z = COL_Q + D_ATT
COL_G = COL_Z + D_INNER
COL_K = COL_G + 2 * D_MODEL
PROJ_COLS = COL_K + D_ATT
PROJ_TILE = 1024
DT_PAD = LANES


def _vmem_limit(nbytes):
    return int(min(nbytes, V7X_VMEM_BYTES - 4 * 1024 * 1024))


def _dot(a, b):
    return jnp.dot(a, b, preferred_element_type=F32)


def _dot_nt(a, b):
    return lax.dot_general(a, b, (((1,), (1,)), ((), ())), preferred_element_type=F32)


def _dot_tn(a, b):
    return lax.dot_general(a, b, (((0,), (0,)), ((), ())), preferred_element_type=F32)


def _split2(x):
    hi = x.astype(BF16)
    lo = (x - hi.astype(F32)).astype(BF16)
    return hi, lo


def _split3(x):
    hi = x.astype(BF16)
    r = x - hi.astype(F32)
    mid = r.astype(BF16)
    lo = (r - mid.astype(F32)).astype(BF16)
    return hi, mid, lo


def _dot_f32(a, b):
    a_hi, a_lo = _split2(a)
    b_hi, b_lo = _split2(b)
    return _dot(a_hi, b_hi) + _dot(a_hi, b_lo) + _dot(a_lo, b_hi)


def _sigmoid(x):
    return 1.0 / (1.0 + jnp.exp2(x * -LOG2_E))


def _softplus(x):
    return jnp.maximum(x, 0.0) + jnp.log1p(jnp.exp(-jnp.abs(x)))


def _in_proj_kernel(x_ref, nw_ref, w_ref, wdt_hi_ref, wdt_lo_ref, proj_ref, v_ref, dt_ref, h_ref, *, n_col):
    j = pl.program_id(1)

    @pl.when(j == 0)
    def _():
        x = x_ref[...]
        ms = jnp.mean(x * x, axis=-1, keepdims=True)
        h = x * lax.rsqrt(ms + RMS_EPS) * nw_ref[...]
        h_hi, h_lo = _split2(h)
        h_ref[...] = h_hi
        dt_ref[...] = (_dot(h_hi, wdt_hi_ref[...]) + _dot(h_hi, wdt_lo_ref[...])
                       + _dot(h_lo, wdt_hi_ref[...]))

    acc = _dot(h_ref[...], w_ref[...])

    @pl.when(j < n_col - 1)
    def _():
        proj_ref[...] = acc

    @pl.when(j == n_col - 1)
    def _():
        v_ref[...] = acc


def _in_proj(x2d, norm_w, w_main, wdt_hi, wdt_lo, tm):
    n = x2d.shape[0]
    n_col = w_main.shape[1] // PROJ_TILE
    n_proj = PROJ_COLS // PROJ_TILE
    vmem = (2 * tm * D_MODEL * 4 + 2 * D_MODEL * PROJ_TILE * 2 + 4 * tm * PROJ_TILE * 4
            + 2 * tm * DT_PAD * 4 + tm * D_MODEL * 2 + 3 * tm * PROJ_TILE * 4 + (4 << 20))
    return pl.pallas_call(
        functools.partial(_in_proj_kernel, n_col=n_col),
        grid=(n // tm, n_col),
        in_specs=[
            pl.BlockSpec((tm, D_MODEL), lambda i, j: (i, 0)),
            pl.BlockSpec((1, D_MODEL), lambda i, j: (0, 0)),
            pl.BlockSpec((D_MODEL, PROJ_TILE), lambda i, j: (0, j)),
            pl.BlockSpec((D_MODEL, DT_PAD), lambda i, j: (0, 0)),
            pl.BlockSpec((D_MODEL, DT_PAD), lambda i, j: (0, 0)),
        ],
        out_specs=[
            pl.BlockSpec((tm, PROJ_TILE), lambda i, j: (i, jnp.minimum(j, n_proj - 1))),
            pl.BlockSpec((tm, PROJ_TILE), lambda i, j: (i, 0)),
            pl.BlockSpec((tm, DT_PAD), lambda i, j: (i, 0)),
        ],
        out_shape=[
            jax.ShapeDtypeStruct((n, PROJ_COLS), F32),
            jax.ShapeDtypeStruct((n, D_ATT), F32),
            jax.ShapeDtypeStruct((n, DT_PAD), F32),
        ],
        scratch_shapes=[pltpu.VMEM((tm, D_MODEL), BF16)],
        compiler_params=pltpu.CompilerParams(
            dimension_semantics=("arbitrary", "arbitrary"), vmem_limit_bytes=_vmem_limit(vmem)),
        name="in_proj",
    )(x2d, norm_w, w_main, wdt_hi, wdt_lo)


def _ssd_kernel(xbc_ref, z_ref, dt_ref, convp_ref, ssmp_ref, convw_ref, convb_ref, dtb_ref, alog_ref,
                dskip_ref, normw_ref, tri_ref, expand_ref, expand_t_ref,
                y_ref, ssm_out_ref, h_ref, ext_ref, ybuf_ref, *, L, T, n_chunks):
    c = pl.program_id(1)
    gn = SSM_GROUPS * SSM_STATE
    hpg = SSM_HEADS // SSM_GROUPS
    gw = hpg * SSM_HEAD_DIM

    @pl.when(c == 0)
    def _():
        h_ref[...] = ssmp_ref[0]
        ext_ref[0:SUBLANES, :] = convp_ref[0]

    ext_ref[SUBLANES:SUBLANES + T, :] = xbc_ref[0]
    if T < L:
        ext_ref[SUBLANES + T:SUBLANES + L, :] = jnp.zeros((L - T, CONV_DIM), F32)

    cw = convw_ref[...]
    conv = convb_ref[...] + cw[3:4, :] * ext_ref[SUBLANES:SUBLANES + L, :]
    conv = conv + cw[2:3, :] * ext_ref[SUBLANES - 1:SUBLANES - 1 + L, :]
    conv = conv + cw[1:2, :] * ext_ref[SUBLANES - 2:SUBLANES - 2 + L, :]
    conv = conv + cw[0:1, :] * ext_ref[SUBLANES - 3:SUBLANES - 3 + L, :]
    ext_ref[0:SUBLANES, :] = ext_ref[L:L + SUBLANES, :]

    xbc = conv * _sigmoid(conv)
    xs = xbc[:, :D_INNER]
    bm = xbc[:, D_INNER:D_INNER + gn]
    cm = xbc[:, D_INNER + gn:]

    row_l = lax.broadcasted_iota(jnp.int32, (L, DT_PAD), 0)
    if T < L:
        dt_raw = jnp.concatenate([dt_ref[0], jnp.zeros((L - T, DT_PAD), F32)], axis=0)
    else:
        dt_raw = dt_ref[0]
    dt = jnp.where(row_l < T, _softplus(dt_raw + dtb_ref[...]), 0.0)
    a = -jnp.exp(alog_ref[...])
    dta = dt * a
    d_hi, d_mid, d_lo = _split3(dta)
    tri = tri_ref[...]
    acum = _dot(tri, d_hi) + _dot(tri, d_mid) + _dot(tri, d_lo)
    acum2 = acum * LOG2_E
    acum2_t = acum2.T
    last = acum[L - 1:L, :]
    e_last = jnp.exp(last)
    stacked = jnp.concatenate([dt, jnp.exp(acum), jnp.exp(last - acum) * dt], axis=0)
    s_hi, s_lo = _split2(stacked)
    ex = expand_ref[...]
    wide = _dot(s_hi, ex) + _dot(s_lo, ex)
    dt_x = wide[0:L]
    ea_x = wide[L:2 * L]
    elm_x = wide[2 * L:3 * L]
    xdt_b = (xs * dt_x).astype(BF16)
    xdec_b = (xs * elm_x).astype(BF16)

    row = lax.broadcasted_iota(jnp.int32, (L, L), 0)
    col = lax.broadcasted_iota(jnp.int32, (L, L), 1)
    causal = row >= col
    lane = lax.broadcasted_iota(jnp.int32, (L, 2 * SSM_HEAD_DIM), 1)
    first_head = lane < SSM_HEAD_DIM

    for g in range(SSM_GROUPS):
        bg = bm[:, g * SSM_STATE:(g + 1) * SSM_STATE].astype(BF16)
        cg = cm[:, g * SSM_STATE:(g + 1) * SSM_STATE].astype(BF16)
        cb = jnp.where(causal, _dot_nt(cg, bg), 0.0)
        for pr in range(hpg // 2):
            ha = g * hpg + 2 * pr
            c0 = ha * SSM_HEAD_DIM
            xp = xdt_b[:, c0:c0 + 2 * SSM_HEAD_DIM]
            ys = []
            for hh in (ha, ha + 1):
                dec = jnp.exp2(jnp.minimum(acum2[:, hh:hh + 1] - acum2_t[hh:hh + 1, :], 0.0))
                ys.append(_dot((cb * dec).astype(BF16), xp))
            ybuf_ref[:, c0:c0 + 2 * SSM_HEAD_DIM] = jnp.where(first_head, ys[0], ys[1])
        r0 = g * gw
        hg = h_ref[r0:r0 + gw, :]
        y_state = _dot_nt(cg, hg.astype(BF16)) * ea_x[:, r0:r0 + gw]
        ybuf_ref[:, r0:r0 + gw] = ybuf_ref[:, r0:r0 + gw] + y_state
        upd = _dot_tn(xdec_b[:, r0:r0 + gw], bg)
        el_col = jnp.sum(expand_t_ref[r0:r0 + gw, :] * e_last, axis=1, keepdims=True)
        h_ref[r0:r0 + gw, :] = hg * el_col + upd

    y = ybuf_ref[...] + dskip_ref[...] * xs
    y = y[0:T]
    z = z_ref[0]
    y = y * (z * _sigmoid(z))
    for g in range(SSM_GROUPS):
        yg = y[:, g * gw:(g + 1) * gw]
        ms = jnp.mean(yg * yg, axis=-1, keepdims=True)
        y_ref[0, :, g * gw:(g + 1) * gw] = (yg * lax.rsqrt(ms + RMS_EPS) * normw_ref[:, g * gw:(g + 1) * gw]).astype(BF16)

    @pl.when(c == n_chunks - 1)
    def _():
        ssm_out_ref[0] = h_ref[...]


def _ssd(proj3, dt3, conv_prev8, ssm_prev, conv_w8, conv_b, dt_bias, a_log, d_skip_x, norm_w, L, T):
    b, t_total, _ = proj3.shape
    n_chunks = t_total // T
    tri = (jnp.arange(L)[:, None] >= jnp.arange(L)[None, :]).astype(BF16)
    head_of = jnp.arange(D_INNER) // SSM_HEAD_DIM
    expand = (jnp.arange(DT_PAD)[:, None] == head_of[None, :]).astype(BF16)
    expand_t = (head_of[:, None] == jnp.arange(DT_PAD)[None, :]).astype(F32)
    const = lambda bi, ci: (0, 0)
    vmem = (2 * T * CONV_DIM * 4 + 2 * T * D_INNER * 4 + 4 * D_INNER * SSM_STATE * 4 + (L + SUBLANES) * CONV_DIM * 4
            + L * D_INNER * 4 + 2 * T * D_INNER * 2 + 2 * (DT_PAD * D_INNER * 2 + D_INNER * DT_PAD * 4)
            + 14 * L * D_INNER * 4 + (8 << 20))
    return pl.pallas_call(
        functools.partial(_ssd_kernel, L=L, T=T, n_chunks=n_chunks),
        grid=(b, n_chunks),
        in_specs=[
            pl.BlockSpec((1, T, CONV_DIM), lambda bi, ci: (bi, ci, COL_XBC // CONV_DIM)),
            pl.BlockSpec((1, T, D_INNER), lambda bi, ci: (bi, ci, COL_Z // D_INNER)),
            pl.BlockSpec((1, T, DT_PAD), lambda bi, ci: (bi, ci, 0)),
            pl.BlockSpec((1, SUBLANES, CONV_DIM), lambda bi, ci: (bi, 0, 0)),
            pl.BlockSpec((1, D_INNER, SSM_STATE), lambda bi, ci: (bi, 0, 0)),
            pl.BlockSpec((SUBLANES, CONV_DIM), const),
            pl.BlockSpec((1, CONV_DIM), const),
            pl.BlockSpec((1, DT_PAD), const),
            pl.BlockSpec((1, DT_PAD), const),
            pl.BlockSpec((1, D_INNER), const),
            pl.BlockSpec((1, D_INNER), const),
            pl.BlockSpec((L, L), const),
            pl.BlockSpec((DT_PAD, D_INNER), const),
            pl.BlockSpec((D_INNER, DT_PAD), const),
        ],
        out_specs=[
            pl.BlockSpec((1, T, D_INNER), lambda bi, ci: (bi, ci, 0)),
            pl.BlockSpec((1, D_INNER, SSM_STATE), lambda bi, ci: (bi, 0, 0)),
        ],
        out_shape=[
            jax.ShapeDtypeStruct((b, t_total, D_INNER), BF16),
            jax.ShapeDtypeStruct((b, D_INNER, SSM_STATE), F32),
        ],
        scratch_shapes=[
            pltpu.VMEM((D_INNER, SSM_STATE), F32),
            pltpu.VMEM((L + SUBLANES, CONV_DIM), F32),
            pltpu.VMEM((L, D_INNER), F32),
        ],
        compiler_params=pltpu.CompilerParams(
            dimension_semantics=("arbitrary", "arbitrary"), vmem_limit_bytes=_vmem_limit(vmem)),
        name="ssd",
    )(proj3, proj3, dt3, conv_prev8, ssm_prev, conv_w8, conv_b, dt_bias, a_log, d_skip_x, norm_w,
      tri, expand, expand_t)


FLASH_CHUNK = 4


def _norm_rope_t(x_t, w_col, cos, sin):
    outs = []
    for h in range(2):
        xh = x_t[h * ATT_HEAD_DIM:(h + 1) * ATT_HEAD_DIM, :]
        ms = jnp.mean(xh * xh, axis=0, keepdims=True)
        xn = xh * lax.rsqrt(ms + RMS_EPS) * w_col[h * ATT_HEAD_DIM:(h + 1) * ATT_HEAD_DIM, :]
        x1 = xn[0:ROPE_HALF, :]
        x2 = xn[ROPE_HALF:ROPE_DIM, :]
        outs += [x1 * cos - x2 * sin, x2 * cos + x1 * sin, xn[ROPE_DIM:, :]]
    return jnp.concatenate(outs, axis=0)


def _moba_prompt_kernel(q_ref, k_ref, v_ref, qw_ref, kw_ref, cos_ref, sin_ref, y_ref, kout_ref, vout_ref,
                        qt_ref, qtf_ref, kn_ref, vtc_ref, vtb_ref, km_ref, s_ref, mx_ref, m_ref, l_ref, acc_ref,
                        m0_ref, l0_ref, acc0_ref, *, n_blocks):
    blk = MOBA_BLOCK
    cw = FLASH_CHUNK
    n_chunks = n_blocks // cw
    hd = ATT_HEAD_DIM
    pair_w = 2 * hd
    scale = (ATT_HEAD_DIM ** -0.5) * LOG2_E
    lane_k = lax.broadcasted_iota(jnp.int32, (blk, pair_w), 1)

    for i in range(n_blocks):
        r0 = i * blk
        c0 = (i % cw) * blk
        cos = cos_ref[i]
        sin = sin_ref[i]
        qn = _norm_rope_t(q_ref[0, r0:r0 + blk, :].T, qw_ref[...], cos, sin)
        qtf_ref[i] = qn
        qs = (qn * scale).astype(BF16)
        qt_ref[0, i, 0:hd, :] = qs[0:hd, :]
        qt_ref[1, i, 0:hd, :] = qs[hd:pair_w, :]
        kn_t = _norm_rope_t(k_ref[0, r0:r0 + blk, :].T, kw_ref[...], cos, sin)
        kout_ref[0, :, r0:r0 + blk] = kn_t
        kn = kn_t.T
        km_ref[i:i + 1, :] = jnp.mean(kn, axis=0, keepdims=True)
        onehot = (lane_k == hd + i).astype(F32)
        kn_ref[0, i // cw, c0:c0 + blk, :] = jnp.where(lane_k < hd, kn, onehot).astype(BF16)
        kn_ref[1, i // cw, c0:c0 + blk, :] = jnp.where(lane_k < hd, pltpu.roll(kn, hd, 1), onehot).astype(BF16)
        vt = v_ref[0, r0:r0 + blk, :].T
        vout_ref[0, :, r0:r0 + blk] = vt
        vt = vt.astype(BF16)
        vtb_ref[i] = vt
        vtc_ref[i // cw, :, c0:c0 + blk] = vt

    km = km_ref[...]
    km_lane = lax.broadcasted_iota(jnp.int32, km.shape, 1)
    km_heads = (jnp.where(km_lane < hd, km, 0.0), jnp.where(km_lane < hd, 0.0, km))
    jrow = lax.broadcasted_iota(jnp.int32, (n_blocks, blk), 0)
    k_sel = min(MOBA_TOPK, n_blocks)
    pad_rows = jnp.zeros((hd - n_blocks, blk), BF16)

    for i in range(n_blocks):
        qtf = qtf_ref[i]
        for h in range(2):
            gate = _dot_f32(km_heads[h], qtf)
            valid = jrow < i
            gate = jnp.where(valid, gate, NEG_INF)
            cnt = jnp.zeros((n_blocks, blk), jnp.int32)
            for jp in range(i):
                gj = gate[jp:jp + 1, :]
                better = (gj > gate) | ((gj == gate) & (jp < jrow))
                cnt = cnt + better.astype(jnp.int32)
            sel = valid & (cnt < k_sel)
            bias = jnp.where(sel, 0.0, MASK_BIAS).astype(BF16)
            qt_ref[h, i, hd:pair_w, :] = jnp.concatenate([bias, pad_rows], axis=0)

    key_i = lax.broadcasted_iota(jnp.int32, (blk, blk), 0)
    qry_i = lax.broadcasted_iota(jnp.int32, (blk, blk), 1)
    no_bias = jnp.zeros((hd, blk), BF16)
    for i in range(n_blocks):
        c0 = (i % cw) * blk
        s_own = [_dot(kn_ref[h, i // cw, c0:c0 + blk, :], jnp.concatenate([qt_ref[h, i, 0:hd, :], no_bias], axis=0))
                 for h in range(2)]
        for h in range(2):
            s = jnp.where(key_i <= qry_i, s_own[h], NEG_INF)
            m = jnp.max(s, axis=0, keepdims=True)
            p = jnp.exp2(s - m)
            m0_ref[h, i] = m
            l0_ref[h, i] = jnp.sum(p, axis=0, keepdims=True)
            acc0_ref[h, i] = _dot(vtb_ref[i, h * hd:(h + 1) * hd, :], p.astype(BF16))

    mx_ref[...] = jnp.full(mx_ref.shape, NEG_INF, F32)

    def score_tile(c, buf, h, u, qt):
        t = _dot(kn_ref[h, c, u * blk:(u + 1) * blk, :], qt)
        s_ref[buf, h, u * blk:(u + 1) * blk, :] = t
        mx_ref[buf, h, u:u + 1, :] = jnp.max(t, axis=0, keepdims=True)

    def chunk_scores(c, buf, h, qt):
        for u in range(cw):
            score_tile(c, buf, h, u, qt)

    if n_blocks > 1:
        for h in range(2):
            chunk_scores(0, 0, h, qt_ref[h, 1])

    def attend(i, carry):
        r0 = pl.multiple_of(i * blk, blk)
        qts = [qt_ref[h, i] for h in range(2)]
        for h in range(2):
            m_ref[h] = m0_ref[h, i]
            l_ref[h] = l0_ref[h, i]
            acc_ref[h] = acc0_ref[h, i]

        def region(c, with_next):
            slot = c % 2
            m1s, alphas, l1s, pvs = [], [], [], [None, None]
            for h in range(2):
                m0 = m_ref[h]
                m1 = m0
                for u in range(cw):
                    m1 = jnp.maximum(m1, mx_ref[slot, h, u:u + 1, :])
                m1s.append(m1)
                alphas.append(jnp.exp2(m0 - m1))
                l1s.append(alphas[h] * l_ref[h])
            for u in range(cw):
                for h in range(2):
                    if with_next:
                        score_tile(c + 1, 1 - slot, h, u, qts[h])
                    p = jnp.exp2(s_ref[slot, h, u * blk:(u + 1) * blk, :] - m1s[h])
                    l1s[h] = l1s[h] + jnp.sum(p, axis=0, keepdims=True)
                    d = _dot(vtc_ref[c, h * hd:(h + 1) * hd, u * blk:(u + 1) * blk], p.astype(BF16))
                    pvs[h] = d if pvs[h] is None else pvs[h] + d
            for h in range(2):
                m_ref[h] = m1s[h]
                l_ref[h] = l1s[h]
                acc_ref[h] = alphas[h] * acc_ref[h] + pvs[h]

        for c in range(n_chunks):
            if c + 1 < n_chunks:
                pl.when((c + 1) * cw < i)(functools.partial(region, c, True))
                pl.when((c * cw < i) & ((c + 1) * cw >= i))(functools.partial(region, c, False))
            else:
                pl.when(c * cw < i)(functools.partial(region, c, False))

        o_t = jnp.concatenate([acc_ref[0] / l_ref[0], acc_ref[1] / l_ref[1]], axis=0)
        y_ref[0, pl.ds(r0, blk), :] = o_t.T.astype(BF16)
        nxt = jnp.minimum(i + 1, n_blocks - 1)
        for h in range(2):
            chunk_scores(0, 0, h, qt_ref[h, nxt])
        return carry

    lax.fori_loop(0, n_blocks, attend, 0)


def _moba_prompt(proj3, v3, q_norm_w, k_norm_w, cos_t, sin_t):
    b, s, _ = proj3.shape
    n_blocks = s // MOBA_BLOCK
    assert n_blocks % FLASH_CHUNK == 0 and n_blocks <= ATT_HEAD_DIM
    n_chunks = n_blocks // FLASH_CHUNK
    n_pairs = ATT_HEADS // 2
    pw = 2 * ATT_HEAD_DIM
    cblk = FLASH_CHUNK * MOBA_BLOCK
    qw = jnp.tile(q_norm_w, 2).reshape(pw, 1)
    kw = jnp.tile(k_norm_w, 2).reshape(pw, 1)
    const3 = lambda bi, p: (0, 0, 0)
    vmem = (2 * 3 * s * pw * 4 + 2 * s * pw * 2 + 4 * s * pw * 4 + 7 * s * pw * 2 + s * pw * 4
            + 4 * cblk * MOBA_BLOCK * 4 + (16 << 20))
    y_att, k_t, v_t = pl.pallas_call(
        functools.partial(_moba_prompt_kernel, n_blocks=n_blocks),
        grid=(b, n_pairs),
        in_specs=[
            pl.BlockSpec((1, s, pw), lambda bi, p: (bi, 0, COL_Q // pw + p)),
            pl.BlockSpec((1, s, pw), lambda bi, p: (bi, 0, COL_K // pw + p)),
            pl.BlockSpec((1, s, pw), lambda bi, p: (bi, 0, p)),
            pl.BlockSpec((pw, 1), lambda bi, p: (0, 0)),
            pl.BlockSpec((pw, 1), lambda bi, p: (0, 0)),
            pl.BlockSpec((n_blocks, ROPE_HALF, MOBA_BLOCK), const3),
            pl.BlockSpec((n_blocks, ROPE_HALF, MOBA_BLOCK), const3),
        ],
        out_specs=[
            pl.BlockSpec((1, s, pw), lambda bi, p: (bi, 0, p)),
            pl.BlockSpec((1, pw, s), lambda bi, p: (bi, p, 0)),
            pl.BlockSpec((1, pw, s), lambda bi, p: (bi, p, 0)),
        ],
        out_shape=[
            jax.ShapeDtypeStruct((b, s, D_ATT), BF16),
            jax.ShapeDtypeStruct((b, D_ATT, s), F32),
            jax.ShapeDtypeStruct((b, D_ATT, s), F32),
        ],
        scratch_shapes=[
            pltpu.VMEM((2, n_blocks, pw, MOBA_BLOCK), BF16),
            pltpu.VMEM((n_blocks, pw, MOBA_BLOCK), F32),
            pltpu.VMEM((2, n_chunks, cblk, pw), BF16),
            pltpu.VMEM((n_chunks, pw, cblk), BF16),
            pltpu.VMEM((n_blocks, pw, MOBA_BLOCK), BF16),
            pltpu.VMEM((n_blocks, pw), F32),
            pltpu.VMEM((2, 2, cblk, MOBA_BLOCK), F32),
            pltpu.VMEM((2, 2, SUBLANES, MOBA_BLOCK), F32),
            pltpu.VMEM((2, 1, MOBA_BLOCK), F32),
            pltpu.VMEM((2, 1, MOBA_BLOCK), F32),
            pltpu.VMEM((2, ATT_HEAD_DIM, MOBA_BLOCK), F32),
            pltpu.VMEM((2, n_blocks, 1, MOBA_BLOCK), F32),
            pltpu.VMEM((2, n_blocks, 1, MOBA_BLOCK), F32),
            pltpu.VMEM((2, n_blocks, ATT_HEAD_DIM, MOBA_BLOCK), F32),
        ],
        compiler_params=pltpu.CompilerParams(
            dimension_semantics=("arbitrary", "arbitrary"), vmem_limit_bytes=_vmem_limit(vmem)),
        name="moba_prompt",
    )(proj3, proj3, v3, qw, kw, cos_t, sin_t)
    to_heads = lambda t: t.reshape(b, ATT_HEADS, ATT_HEAD_DIM, s).transpose(0, 3, 1, 2)
    return y_att, to_heads(k_t), to_heads(v_t)


def _seg_sum(x, seg):
    w = x.shape[-1]
    lane = lax.broadcasted_iota(jnp.int32, x.shape, x.ndim - 1)
    s = 1
    while s < seg:
        x = x + jnp.where((lane & s) != 0, pltpu.roll(x, s, x.ndim - 1), pltpu.roll(x, w - s, x.ndim - 1))
        s *= 2
    return x


def _norm_rope_rows(x, w_row, cosf, sin_lo, sin_hi):
    ms = _seg_sum(x * x, ATT_HEAD_DIM) * (1.0 / ATT_HEAD_DIM)
    xn = x * lax.rsqrt(ms + RMS_EPS) * w_row
    width = x.shape[-1]
    return (xn * cosf + pltpu.roll(xn, width - ROPE_HALF, 1) * sin_lo + pltpu.roll(xn, ROPE_HALF, 1) * sin_hi)


def _moba_sample_kernel(pt_ref, q_ref, k_ref, v_ref, qw_ref, kw_ref, cosf_ref, slo_ref, shi_ref, hmask_ref,
                        *rest, T, pages_per_step, n_steps, n_blocks):
    kp = rest[:pages_per_step]
    vp = rest[pages_per_step:2 * pages_per_step]
    y_ref, kout_ref = rest[2 * pages_per_step:2 * pages_per_step + 2]
    qbd_ref, knp_ref, vnp_ref, ms_ref, ls_ref, gs_ref, o_ref = rest[2 * pages_per_step + 2:]
    step = pl.program_id(1)
    rows = ATT_HEADS * T
    scale = ATT_HEAD_DIM ** -0.5
    hmask = hmask_ref[...]

    def fold(o_all):
        o_all = o_all * hmask
        acc = o_all[:, 0:LANES]
        for u in range(1, D_ATT // LANES):
            acc = acc + o_all[:, u * LANES:(u + 1) * LANES]
        return acc

    @pl.when(step == 0)
    def _():
        qn = _norm_rope_rows(q_ref[0], qw_ref[...], cosf_ref[...], slo_ref[...], shi_ref[...])
        kn = _norm_rope_rows(k_ref[0], kw_ref[...], cosf_ref[...], slo_ref[...], shi_ref[...])
        kout_ref[0] = kn
        q_rows = jnp.concatenate([qn * scale] * ATT_HEADS, axis=0) * hmask
        q_hi, q_lo = _split2(q_rows)
        qbd_ref[0:rows, :] = q_hi
        qbd_ref[rows:2 * rows, :] = q_lo
        pad = jnp.zeros((rows - T, D_ATT), F32)
        knp_ref[...] = jnp.concatenate([kn, pad], axis=0).astype(BF16)
        vnp_ref[...] = jnp.concatenate([v_ref[0], pad], axis=0).astype(BF16)
        ms_ref[...] = jnp.full((rows, LANES), NEG_INF, F32)
        gs_ref[...] = jnp.full((rows, LANES), NEG_INF, F32)
        ls_ref[...] = jnp.zeros((rows, LANES), F32)

    lane = lax.broadcasted_iota(jnp.int32, (rows, LANES), 1)
    qbd = qbd_ref[...]
    blocks_per_step = pages_per_step // PAGES_PER_BLOCK
    for jj in range(blocks_per_step):
        j = step * blocks_per_step + jj
        kt = jnp.concatenate([kp[jj * PAGES_PER_BLOCK + u][0, 0].reshape(D_ATT, PAGE_SIZE)
                              for u in range(PAGES_PER_BLOCK)], axis=1).astype(BF16)
        vt = jnp.concatenate([vp[jj * PAGES_PER_BLOCK + u][0, 0].reshape(D_ATT, PAGE_SIZE)
                              for u in range(PAGES_PER_BLOCK)], axis=1).astype(BF16)
        s2 = _dot(qbd, kt)
        s = s2[0:rows, :] + s2[rows:2 * rows, :]
        g_j = jnp.mean(s, axis=1, keepdims=True)
        m_j = jnp.max(s, axis=1, keepdims=True)
        p = jnp.exp(s - m_j)
        l_j = jnp.sum(p, axis=1, keepdims=True)
        o_ref[j] = fold(_dot_nt(p.astype(BF16), vt))
        here = lane == j
        ms_ref[...] = jnp.where(here, m_j, ms_ref[...])
        ls_ref[...] = jnp.where(here, l_j, ls_ref[...])
        gs_ref[...] = jnp.where(here, g_j, gs_ref[...])

    @pl.when(step == n_steps - 1)
    def _():
        r = lax.broadcasted_iota(jnp.int32, (rows, LANES), 0)
        s2 = _dot_nt(qbd, knp_ref[...])
        s_own = s2[0:rows, :] + s2[rows:2 * rows, :]
        s_own = jnp.where((lane < T) & (lane <= (r % T)), s_own, NEG_INF)
        m_o = jnp.max(s_own, axis=1, keepdims=True)
        p_o = jnp.exp(s_own - m_o)
        l_o = jnp.sum(p_o, axis=1, keepdims=True)
        o_o = fold(_dot(p_o.astype(BF16), vnp_ref[...]))

        gs = gs_ref[...]
        cnt = jnp.zeros((rows, LANES), jnp.int32)
        for jp in range(n_blocks):
            gj = gs[:, jp:jp + 1]
            better = (gj > gs) | ((gj == gs) & (jp < lane))
            cnt = cnt + better.astype(jnp.int32)
        sel = (lane < n_blocks) & (cnt < min(MOBA_TOPK, n_blocks))
        ms = ms_ref[...]
        m_tot = jnp.maximum(jnp.max(jnp.where(sel, ms, NEG_INF), axis=1, keepdims=True), m_o)
        w = jnp.where(sel, jnp.exp(ms - m_tot), 0.0)
        w_o = jnp.exp(m_o - m_tot)
        l_tot = jnp.sum(w * ls_ref[...], axis=1, keepdims=True) + w_o * l_o
        acc = w_o * o_o
        for jp in range(n_blocks):
            acc = acc + w[:, jp:jp + 1] * o_ref[jp]
        out = acc / l_tot
        even_head = lax.broadcasted_iota(jnp.int32, (T, LANES), 1) < ATT_HEAD_DIM
        y_ref[0] = jnp.concatenate(
            [jnp.where(even_head, out[2 * hp * T:(2 * hp + 1) * T, :], out[(2 * hp + 1) * T:(2 * hp + 2) * T, :])
             for hp in range(ATT_HEADS // 2)], axis=1).astype(BF16)


def _moba_sample(q3, k3, v3, cache_k, cache_v, page_table, layer, q_norm_w, k_norm_w, cosf, sin_lo, sin_hi):
    nseq, T, _ = v3.shape
    n_pages = page_table.shape[1]
    assert T == SUBLANES and n_pages % PAGES_PER_BLOCK == 0
    n_blocks = n_pages // PAGES_PER_BLOCK
    assert n_blocks <= LANES
    pages_per_step = 16 if n_pages % 16 == 0 else PAGES_PER_BLOCK
    assert n_pages % pages_per_step == 0
    n_steps = n_pages // pages_per_step
    rows = ATT_HEADS * T
    hmask = ((jnp.arange(rows)[:, None] // T) == (jnp.arange(D_ATT)[None, :] // ATT_HEAD_DIM)).astype(F32)
    qw = jnp.tile(q_norm_w, ATT_HEADS).reshape(1, D_ATT)
    kw = jnp.tile(k_norm_w, ATT_HEADS).reshape(1, D_ATT)
    ck_t = jnp.transpose(cache_k, (0, 1, 3, 4, 2))
    cv_t = jnp.transpose(cache_v, (0, 1, 3, 4, 2))

    tok = lambda b, s, pt: (b, 0, 0)
    const = lambda b, s, pt: (0, 0)
    page_shape = (1, 1, ATT_HEADS, ATT_HEAD_DIM, PAGE_SIZE)

    def page_spec(u):
        return pl.BlockSpec(page_shape, lambda b, s, pt, u=u: (layer, pt[b, s * pages_per_step + u], 0, 0, 0))

    in_specs = [
        pl.BlockSpec((1, T, D_ATT), lambda b, s, pt: (b, 0, COL_Q // D_ATT)),
        pl.BlockSpec((1, T, D_ATT), lambda b, s, pt: (b, 0, COL_K // D_ATT)),
        pl.BlockSpec((1, T, D_ATT), tok),
        pl.BlockSpec((1, D_ATT), const),
        pl.BlockSpec((1, D_ATT), const),
        pl.BlockSpec((T, D_ATT), const),
        pl.BlockSpec((T, D_ATT), const),
        pl.BlockSpec((T, D_ATT), const),
        pl.BlockSpec((rows, D_ATT), const),
    ] + [page_spec(u) for u in range(pages_per_step)] * 2
    page_vmem = PAGE_SIZE * D_ATT * 4
    vmem = 2 * 2 * pages_per_step * page_vmem + n_blocks * rows * LANES * 4 + (24 << 20)
    return pl.pallas_call(
        functools.partial(_moba_sample_kernel, T=T, pages_per_step=pages_per_step, n_steps=n_steps,
                          n_blocks=n_blocks),
        grid_spec=pltpu.PrefetchScalarGridSpec(
            num_scalar_prefetch=1,
            grid=(nseq, n_steps),
            in_specs=in_specs,
            out_specs=[pl.BlockSpec((1, T, D_ATT), tok), pl.BlockSpec((1, T, D_ATT), tok)],
            scratch_shapes=[
                pltpu.VMEM((2 * rows, D_ATT), BF16),
                pltpu.VMEM((rows, D_ATT), BF16),
                pltpu.VMEM((rows, D_ATT), BF16),
                pltpu.VMEM((rows, LANES), F32),
                pltpu.VMEM((rows, LANES), F32),
                pltpu.VMEM((rows, LANES), F32),
                pltpu.VMEM((n_blocks, rows, LANES), F32),
            ],
        ),
        out_shape=[
            jax.ShapeDtypeStruct((nseq, T, D_ATT), BF16),
            jax.ShapeDtypeStruct((nseq, T, D_ATT), F32),
        ],
        compiler_params=pltpu.CompilerParams(
            dimension_semantics=("arbitrary", "arbitrary"), vmem_limit_bytes=_vmem_limit(vmem)),
        name="moba_sample",
    )(page_table, q3, k3, v3, qw, kw, cosf, sin_lo, sin_hi, hmask,
      *([ck_t] * pages_per_step), *([cv_t] * pages_per_step))


def _merge_kernel(x_ref, yssm_ref, yatt_ref, g_ref, wssm_ref, watt_ref, wout_ref, n2w_ref, x1_ref, h2_ref):
    u_ssm = _dot(yssm_ref[...], wssm_ref[...])
    u_att = _dot(yatt_ref[...], watt_ref[...])
    g = _sigmoid(g_ref[...])
    merged = (g[:, :D_MODEL] * u_ssm + g[:, D_MODEL:] * u_att).astype(BF16)
    x1 = x_ref[...] + _dot(merged, wout_ref[...])
    x1_ref[...] = x1
    ms = jnp.mean(x1 * x1, axis=-1, keepdims=True)
    h2_ref[...] = (x1 * lax.rsqrt(ms + RMS_EPS) * n2w_ref[...]).astype(BF16)


def _merge(x2d, y_ssm, y_att, proj, w_ssm, w_att, w_out, norm2_w, tm):
    n = x2d.shape[0]
    const = lambda i: (0, 0)
    vmem = (2 * tm * (D_MODEL * 4 + D_INNER * 2 + D_ATT * 2 + 2 * D_MODEL * 4 + D_MODEL * 4 + D_MODEL * 2)
            + 2 * 2 * (D_INNER + D_ATT + D_MODEL) * D_MODEL + 8 * tm * D_MODEL * 4 + (4 << 20))
    return pl.pallas_call(
        _merge_kernel,
        grid=(n // tm,),
        in_specs=[
            pl.BlockSpec((tm, D_MODEL), lambda i: (i, 0)),
            pl.BlockSpec((tm, D_INNER), lambda i: (i, 0)),
            pl.BlockSpec((tm, D_ATT), lambda i: (i, 0)),
            pl.BlockSpec((tm, 2 * D_MODEL), lambda i: (i, COL_G // (2 * D_MODEL))),
            pl.BlockSpec((D_INNER, D_MODEL), const),
            pl.BlockSpec((D_ATT, D_MODEL), const),
            pl.BlockSpec((D_MODEL, D_MODEL), const),
            pl.BlockSpec((1, D_MODEL), const),
        ],
        out_specs=[pl.BlockSpec((tm, D_MODEL), lambda i: (i, 0)), pl.BlockSpec((tm, D_MODEL), lambda i: (i, 0))],
        out_shape=[jax.ShapeDtypeStruct((n, D_MODEL), F32), jax.ShapeDtypeStruct((n, D_MODEL), BF16)],
        compiler_params=pltpu.CompilerParams(
            dimension_semantics=("arbitrary",), vmem_limit_bytes=_vmem_limit(vmem)),
        name="merge",
    )(x2d, y_ssm, y_att, proj, w_ssm, w_att, w_out, norm2_w)


def _ffn_kernel(h2_ref, x1_ref, wg_ref, wu_ref, wd_ref, out_ref):
    f = pl.program_id(1)
    h2 = h2_ref[...]
    gate = _dot(h2, wg_ref[...])
    up = _dot(h2, wu_ref[...])
    act = (gate * _sigmoid(gate) * up).astype(BF16)
    part = _dot(act, wd_ref[...])

    @pl.when(f == 0)
    def _():
        out_ref[...] = x1_ref[...] + part

    @pl.when(f != 0)
    def _():
        out_ref[...] = out_ref[...] + part


def _ffn(h2, x1, w_gate, w_up, w_down, tm, tf):
    n = h2.shape[0]
    vmem = (2 * tm * D_MODEL * (2 + 4 + 4) + 2 * 3 * D_MODEL * tf * 2 + 6 * tm * tf * 4 + (4 << 20))
    return pl.pallas_call(
        _ffn_kernel,
        grid=(n // tm, D_FF // tf),
        in_specs=[
            pl.BlockSpec((tm, D_MODEL), lambda i, f: (i, 0)),
            pl.BlockSpec((tm, D_MODEL), lambda i, f: (i, 0)),
            pl.BlockSpec((D_MODEL, tf), lambda i, f: (0, f)),
            pl.BlockSpec((D_MODEL, tf), lambda i, f: (0, f)),
            pl.BlockSpec((tf, D_MODEL), lambda i, f: (f, 0)),
        ],
        out_specs=pl.BlockSpec((tm, D_MODEL), lambda i, f: (i, 0)),
        out_shape=jax.ShapeDtypeStruct((n, D_MODEL), F32),
        compiler_params=pltpu.CompilerParams(
            dimension_semantics=("arbitrary", "arbitrary"), vmem_limit_bytes=_vmem_limit(vmem)),
        name="ffn",
    )(h2, x1, w_gate, w_up, w_down)


def _rope_angles(pos):
    inv_freq = ROPE_THETA ** (-(jnp.arange(ROPE_HALF, dtype=F32) * 2.0 / ROPE_DIM))
    ang = pos.astype(F32)[:, None] * inv_freq[None, :]
    return jnp.cos(ang), jnp.sin(ang)


def _rope_tables_t(pos, n_blocks):
    cos, sin = _rope_angles(pos)
    to_blocks = lambda t: t.T.reshape(ROPE_HALF, n_blocks, MOBA_BLOCK).transpose(1, 0, 2)
    return to_blocks(cos), to_blocks(sin)


def _rope_tables_rows(pos):
    cos, sin = _rope_angles(pos)
    d = jnp.arange(D_ATT) % ATT_HEAD_DIM
    idx = d % ROPE_HALF
    cosf = jnp.where(d[None, :] < ROPE_DIM, cos[:, idx], 1.0)
    sin_lo = jnp.where(d[None, :] < ROPE_HALF, -sin[:, idx], 0.0)
    sin_hi = jnp.where((d[None, :] >= ROPE_HALF) & (d[None, :] < ROPE_DIM), sin[:, idx], 0.0)
    return cosf.astype(F32), sin_lo.astype(F32), sin_hi.astype(F32)


def _layer_weights(l, norm1_w, w_in, conv_w, conv_b, dt_bias, a_log, d_skip, ssm_norm_w, q_norm_w, k_norm_w,
                   w_ssm_branch, w_att_branch, w_out, norm2_w, w_ffn_in, w_ffn_out):
    w = w_in[l]
    off_xbc = D_INNER
    off_dt = off_xbc + CONV_DIM
    off_q = off_dt + SSM_HEADS
    off_k = off_q + D_ATT
    off_v = off_k + D_ATT
    off_g = off_v + D_ATT
    w_main = jnp.concatenate([w[:, off_xbc:off_dt], w[:, off_q:off_k], w[:, 0:off_xbc], w[:, off_g:],
                              w[:, off_k:off_v], w[:, off_v:off_g]], axis=1).astype(BF16)
    w_dt = jnp.pad(w[:, off_dt:off_q], ((0, 0), (0, DT_PAD - SSM_HEADS)))
    wdt_hi = w_dt.astype(BF16)
    wdt_lo = (w_dt - wdt_hi.astype(F32)).astype(BF16)
    pad_h = (0, DT_PAD - SSM_HEADS)
    return dict(
        norm1_w=norm1_w[l].reshape(1, D_MODEL), w_main=w_main, wdt_hi=wdt_hi, wdt_lo=wdt_lo,
        conv_w8=jnp.pad(conv_w[l], ((0, SUBLANES - CONV_WIDTH), (0, 0))), conv_b=conv_b[l].reshape(1, CONV_DIM),
        dt_bias=jnp.pad(dt_bias[l], pad_h).reshape(1, DT_PAD), a_log=jnp.pad(a_log[l], pad_h).reshape(1, DT_PAD),
        d_skip_x=jnp.repeat(d_skip[l], SSM_HEAD_DIM).reshape(1, D_INNER),
        ssm_norm_w=ssm_norm_w[l].reshape(1, D_INNER), q_norm_w=q_norm_w[l], k_norm_w=k_norm_w[l],
        w_ssm=w_ssm_branch[l].astype(BF16), w_att=w_att_branch[l].astype(BF16), w_out=w_out[l].astype(BF16),
        norm2_w=norm2_w[l].reshape(1, D_MODEL),
        w_gate=w_ffn_in[l][:, :D_FF].astype(BF16), w_up=w_ffn_in[l][:, D_FF:].astype(BF16),
        w_down=w_ffn_out[l].astype(BF16),
    )


def _trunk(x, conv_prev, ssm_prev, wts, attend, ssd_chunk, tm_proj, tm, tf):
    b, t, _ = x.shape
    n = b * t
    x2d = x.reshape(n, D_MODEL)
    proj, v, dt = _in_proj(x2d, wts["norm1_w"], wts["w_main"], wts["wdt_hi"], wts["wdt_lo"], tm_proj)
    proj3 = proj.reshape(b, t, PROJ_COLS)
    v3 = v.reshape(b, t, D_ATT)
    conv_prev8 = jnp.pad(conv_prev, ((0, 0), (SUBLANES - (CONV_WIDTH - 1), 0), (0, 0)))
    L, T = ssd_chunk
    y_ssm, ssm_new = _ssd(proj3, dt.reshape(b, t, DT_PAD), conv_prev8, ssm_prev.reshape(b, D_INNER, SSM_STATE),
                          wts["conv_w8"], wts["conv_b"], wts["dt_bias"], wts["a_log"], wts["d_skip_x"],
                          wts["ssm_norm_w"], L, T)
    y_att, k_heads, v_heads = attend(proj3, v3)
    x1, h2 = _merge(x2d, y_ssm.reshape(n, D_INNER), y_att.reshape(n, D_ATT), proj, wts["w_ssm"], wts["w_att"],
                    wts["w_out"], wts["norm2_w"], tm)
    out = _ffn(h2, x1, wts["w_gate"], wts["w_up"], wts["w_down"], tm, tf)
    conv_new = proj3[:, t - (CONV_WIDTH - 1):, COL_XBC:COL_XBC + CONV_DIM]
    return (out.reshape(b, t, D_MODEL), k_heads, v_heads, conv_new,
            ssm_new.reshape(b, SSM_HEADS, SSM_HEAD_DIM, SSM_STATE))


def kernel(x_prompt, x_sample, cache_k, cache_v, page_table, state_conv, state_ssm, norm1_w, w_in, conv_w, conv_b, dt_bias, a_log, d_skip, ssm_norm_w, q_norm_w, k_norm_w, w_ssm_branch, w_att_branch, w_out, norm2_w, w_ffn_in, w_ffn_out):
    depth = w_in.shape[0]
    bp, sp, _ = x_prompt.shape
    bs, ts, _ = x_sample.shape
    past_len = page_table.shape[1] * PAGE_SIZE
    assert sp % MOBA_BLOCK == 0 and sp % SSD_CHUNK == 0 and ts <= SSD_CHUNK
    cos_t, sin_t = _rope_tables_t(jnp.arange(sp), sp // MOBA_BLOCK)
    cosf, sin_lo, sin_hi = _rope_tables_rows(past_len + jnp.arange(ts))
    xp, xs = x_prompt, x_sample
    outs = [[] for _ in range(8)]
    for l in range(depth):
        wts = _layer_weights(l, norm1_w, w_in, conv_w, conv_b, dt_bias, a_log, d_skip, ssm_norm_w, q_norm_w,
                             k_norm_w, w_ssm_branch, w_att_branch, w_out, norm2_w, w_ffn_in, w_ffn_out)
        attend_p = lambda proj3, v3: _moba_prompt(proj3, v3, wts["q_norm_w"], wts["k_norm_w"], cos_t, sin_t)
        conv0 = jnp.zeros((bp, CONV_WIDTH - 1, CONV_DIM), F32)
        ssm0 = jnp.zeros((bp, SSM_HEADS, SSM_HEAD_DIM, SSM_STATE), F32)
        xp, kp, vp, cp, ssp = _trunk(xp, conv0, ssm0, wts, attend_p, (SSD_CHUNK, SSD_CHUNK), min(1024, bp * sp),
                                     512, D_FF // 2)

        def attend_s(proj3, v3):
            y_att, k3 = _moba_sample(proj3, proj3, v3, cache_k, cache_v, page_table, l, wts["q_norm_w"],
                                     wts["k_norm_w"], cosf, sin_lo, sin_hi)
            heads = (bs, ts, ATT_HEADS, ATT_HEAD_DIM)
            return y_att, k3.reshape(heads), v3.reshape(heads)

        xs, ksn, vsn, cs, sss = _trunk(xs, state_conv[l], state_ssm[l], wts, attend_s, (LANES, ts), bs * ts,
                                       bs * ts, D_FF // 2)
        for lst, val in zip(outs, (kp, vp, ksn, vsn, cp, cs, ssp, sss)):
            lst.append(val)
    return (xp, xs) + tuple(jnp.stack(o) for o in outs)
```

```python
import functools

import jax
import jax.numpy as jnp
from jax import lax
from jax.experimental import pallas as pl
from jax.experimental.pallas import tpu as pltpu

F32 = jnp.float32
BF16 = jnp.bfloat16

D_MODEL = 1024
D_INNER = 2048
SSM_HEADS = 32
SSM_HEAD_DIM = 64
SSM_GROUPS = 4
SSM_STATE = 128
CONV_WIDTH = 4
CONV_DIM = D_INNER + 2 * SSM_GROUPS * SSM_STATE
SSD_CHUNK = 256
ATT_HEADS = 16
ATT_HEAD_DIM = 64
D_ATT = ATT_HEADS * ATT_HEAD_DIM
ROPE_DIM = 16
ROPE_HALF = ROPE_DIM // 2
ROPE_THETA = 500000.0
MOBA_BLOCK = 256
MOBA_TOPK = 3
PAGE_SIZE = 128
PAGES_PER_BLOCK = MOBA_BLOCK // PAGE_SIZE
D_FF = 2816
RMS_EPS = 1e-6
NEG_INF = float("-inf")
MASK_BIAS = -1e30
LOG2_E = 1.4426950408889634

LANES = 128
SUBLANES = 8
V7X_VMEM_BYTES = 64 * 1024 * 1024

COL_XBC = 0
COL_Q = COL_XBC + CONV_DIM
COL_Z = COL_Q + D_ATT
COL_G = COL_Z + D_INNER
COL_K = COL_G + 2 * D_MODEL
PROJ_COLS = COL_K + D_ATT
PROJ_TILE = 1024
DT_PAD = LANES


def _vmem_limit(nbytes):
    return int(min(nbytes, V7X_VMEM_BYTES - 4 * 1024 * 1024))


def _dot(a, b):
    return jnp.dot(a, b, preferred_element_type=F32)


def _dot_nt(a, b):
    return lax.dot_general(a, b, (((1,), (1,)), ((), ())), preferred_element_type=F32)


def _dot_tn(a, b):
    return lax.dot_general(a, b, (((0,), (0,)), ((), ())), preferred_element_type=F32)


def _split2(x):
    hi = x.astype(BF16)
    lo = (x - hi.astype(F32)).astype(BF16)
    return hi, lo


def _split3(x):
    hi = x.astype(BF16)
    r = x - hi.astype(F32)
    mid = r.astype(BF16)
    lo = (r - mid.astype(F32)).astype(BF16)
    return hi, mid, lo


def _dot_f32(a, b):
    a_hi, a_lo = _split2(a)
    b_hi, b_lo = _split2(b)
    return _dot(a_hi, b_hi) + _dot(a_hi, b_lo) + _dot(a_lo, b_hi)


def _sigmoid(x):
    return 1.0 / (1.0 + jnp.exp2(x * -LOG2_E))


def _softplus(x):
    return jnp.maximum(x, 0.0) + jnp.log1p(jnp.exp(-jnp.abs(x)))


def _in_proj_kernel(x_ref, nw_ref, w_ref, wdt_hi_ref, wdt_lo_ref, proj_ref, v_ref, dt_ref, h_ref, *, n_col):
    j = pl.program_id(1)

    @pl.when(j == 0)
    def _():
        x = x_ref[...]
        ms = jnp.mean(x * x, axis=-1, keepdims=True)
        h = x * lax.rsqrt(ms + RMS_EPS) * nw_ref[...]
        h_hi, h_lo = _split2(h)
        h_ref[...] = h_hi
        dt_ref[...] = (_dot(h_hi, wdt_hi_ref[...]) + _dot(h_hi, wdt_lo_ref[...])
                       + _dot(h_lo, wdt_hi_ref[...]))

    acc = _dot(h_ref[...], w_ref[...])

    @pl.when(j < n_col - 1)
    def _():
        proj_ref[...] = acc

    @pl.when(j == n_col - 1)
    def _():
        v_ref[...] = acc


def _in_proj(x2d, norm_w, w_main, wdt_hi, wdt_lo, tm):
    n = x2d.shape[0]
    n_col = w_main.shape[1] // PROJ_TILE
    n_proj = PROJ_COLS // PROJ_TILE
    vmem = (2 * tm * D_MODEL * 4 + 2 * D_MODEL * PROJ_TILE * 2 + 4 * tm * PROJ_TILE * 4
            + 2 * tm * DT_PAD * 4 + tm * D_MODEL * 2 + 3 * tm * PROJ_TILE * 4 + (4 << 20))
    return pl.pallas_call(
        functools.partial(_in_proj_kernel, n_col=n_col),
        grid=(n // tm, n_col),
        in_specs=[
            pl.BlockSpec((tm, D_MODEL), lambda i, j: (i, 0)),
            pl.BlockSpec((1, D_MODEL), lambda i, j: (0, 0)),
            pl.BlockSpec((D_MODEL, PROJ_TILE), lambda i, j: (0, j)),
            pl.BlockSpec((D_MODEL, DT_PAD), lambda i, j: (0, 0)),
            pl.BlockSpec((D_MODEL, DT_PAD), lambda i, j: (0, 0)),
        ],
        out_specs=[
            pl.BlockSpec((tm, PROJ_TILE), lambda i, j: (i, jnp.minimum(j, n_proj - 1))),
            pl.BlockSpec((tm, PROJ_TILE), lambda i, j: (i, 0)),
            pl.BlockSpec((tm, DT_PAD), lambda i, j: (i, 0)),
        ],
        out_shape=[
            jax.ShapeDtypeStruct((n, PROJ_COLS), F32),
            jax.ShapeDtypeStruct((n, D_ATT), F32),
            jax.ShapeDtypeStruct((n, DT_PAD), F32),
        ],
        scratch_shapes=[pltpu.VMEM((tm, D_MODEL), BF16)],
        compiler_params=pltpu.CompilerParams(
            dimension_semantics=("arbitrary", "arbitrary"), vmem_limit_bytes=_vmem_limit(vmem)),
        name="in_proj",
    )(x2d, norm_w, w_main, wdt_hi, wdt_lo)


def _ssd_kernel(xbc_ref, z_ref, dt_ref, convp_ref, ssmp_ref, convw_ref, convb_ref, dtb_ref, alog_ref,
                dskip_ref, normw_ref, tri_ref, expand_ref, expand_t_ref,
                y_ref, ssm_out_ref, h_ref, ext_ref, ybuf_ref, *, L, T, n_chunks):
    c = pl.program_id(1)
    gn = SSM_GROUPS * SSM_STATE
    hpg = SSM_HEADS // SSM_GROUPS
    gw = hpg * SSM_HEAD_DIM

    @pl.when(c == 0)
    def _():
        h_ref[...] = ssmp_ref[0]
        ext_ref[0:SUBLANES, :] = convp_ref[0]

    ext_ref[SUBLANES:SUBLANES + T, :] = xbc_ref[0]
    if T < L:
        ext_ref[SUBLANES + T:SUBLANES + L, :] = jnp.zeros((L - T, CONV_DIM), F32)

    cw = convw_ref[...]
    conv = convb_ref[...] + cw[3:4, :] * ext_ref[SUBLANES:SUBLANES + L, :]
    conv = conv + cw[2:3, :] * ext_ref[SUBLANES - 1:SUBLANES - 1 + L, :]
    conv = conv + cw[1:2, :] * ext_ref[SUBLANES - 2:SUBLANES - 2 + L, :]
    conv = conv + cw[0:1, :] * ext_ref[SUBLANES - 3:SUBLANES - 3 + L, :]
    ext_ref[0:SUBLANES, :] = ext_ref[L:L + SUBLANES, :]

    xbc = conv * _sigmoid(conv)
    xs = xbc[:, :D_INNER]
    bm = xbc[:, D_INNER:D_INNER + gn]
    cm = xbc[:, D_INNER + gn:]

    row_l = lax.broadcasted_iota(jnp.int32, (L, DT_PAD), 0)
    if T < L:
        dt_raw = jnp.concatenate([dt_ref[0], jnp.zeros((L - T, DT_PAD), F32)], axis=0)
    else:
        dt_raw = dt_ref[0]
    dt = jnp.where(row_l < T, _softplus(dt_raw + dtb_ref[...]), 0.0)
    a = -jnp.exp(alog_ref[...])
    dta = dt * a
    d_hi, d_mid, d_lo = _split3(dta)
    tri = tri_ref[...]
    acum = _dot(tri, d_hi) + _dot(tri, d_mid) + _dot(tri, d_lo)
    acum2 = acum * LOG2_E
    acum2_t = acum2.T
    last = acum[L - 1:L, :]
    e_last = jnp.exp(last)
    stacked = jnp.concatenate([dt, jnp.exp(acum), jnp.exp(last - acum) * dt], axis=0)
    s_hi, s_lo = _split2(stacked)
    ex = expand_ref[...]
    wide = _dot(s_hi, ex) + _dot(s_lo, ex)
    dt_x = wide[0:L]
    ea_x = wide[L:2 * L]
    elm_x = wide[2 * L:3 * L]
    xdt_b = (xs * dt_x).astype(BF16)
    xdec_b = (xs * elm_x).astype(BF16)

    row = lax.broadcasted_iota(jnp.int32, (L, L), 0)
    col = lax.broadcasted_iota(jnp.int32, (L, L), 1)
    causal = row >= col
    lane = lax.broadcasted_iota(jnp.int32, (L, 2 * SSM_HEAD_DIM), 1)
    first_head = lane < SSM_HEAD_DIM

    for g in range(SSM_GROUPS):
        bg = bm[:, g * SSM_STATE:(g + 1) * SSM_STATE].astype(BF16)
        cg = cm[:, g * SSM_STATE:(g + 1) * SSM_STATE].astype(BF16)
        cb = jnp.where(causal, _dot_nt(cg, bg), 0.0)
        for pr in range(hpg // 2):
            ha = g * hpg + 2 * pr
            c0 = ha * SSM_HEAD_DIM
            xp = xdt_b[:, c0:c0 + 2 * SSM_HEAD_DIM]
            ys = []
            for hh in (ha, ha + 1):
                dec = jnp.exp2(jnp.minimum(acum2[:, hh:hh + 1] - acum2_t[hh:hh + 1, :], 0.0))
                ys.append(_dot((cb * dec).astype(BF16), xp))
            ybuf_ref[:, c0:c0 + 2 * SSM_HEAD_DIM] = jnp.where(first_head, ys[0], ys[1])
        r0 = g * gw
        hg = h_ref[r0:r0 + gw, :]
        y_state = _dot_nt(cg, hg.astype(BF16)) * ea_x[:, r0:r0 + gw]
        ybuf_ref[:, r0:r0 + gw] = ybuf_ref[:, r0:r0 + gw] + y_state
        upd = _dot_tn(xdec_b[:, r0:r0 + gw], bg)
        el_col = jnp.sum(expand_t_ref[r0:r0 + gw, :] * e_last, axis=1, keepdims=True)
        h_ref[r0:r0 + gw, :] = hg * el_col + upd

    y = ybuf_ref[...] + dskip_ref[...] * xs
    y = y[0:T]
    z = z_ref[0]
    y = y * (z * _sigmoid(z))
    for g in range(SSM_GROUPS):
        yg = y[:, g * gw:(g + 1) * gw]
        ms = jnp.mean(yg * yg, axis=-1, keepdims=True)
        y_ref[0, :, g * gw:(g + 1) * gw] = (yg * lax.rsqrt(ms + RMS_EPS) * normw_ref[:, g * gw:(g + 1) * gw]).astype(BF16)

    @pl.when(c == n_chunks - 1)
    def _():
        ssm_out_ref[0] = h_ref[...]


def _ssd(proj3, dt3, conv_prev8, ssm_prev, conv_w8, conv_b, dt_bias, a_log, d_skip_x, norm_w, L, T):
    b, t_total, _ = proj3.shape
    n_chunks = t_total // T
    tri = (jnp.arange(L)[:, None] >= jnp.arange(L)[None, :]).astype(BF16)
    head_of = jnp.arange(D_INNER) // SSM_HEAD_DIM
    expand = (jnp.arange(DT_PAD)[:, None] == head_of[None, :]).astype(BF16)
    expand_t = (head_of[:, None] == jnp.arange(DT_PAD)[None, :]).astype(F32)
    const = lambda bi, ci: (0, 0)
    vmem = (2 * T * CONV_DIM * 4 + 2 * T * D_INNER * 4 + 4 * D_INNER * SSM_STATE * 4 + (L + SUBLANES) * CONV_DIM * 4
            + L * D_INNER * 4 + 2 * T * D_INNER * 2 + 2 * (DT_PAD * D_INNER * 2 + D_INNER * DT_PAD * 4)
            + 14 * L * D_INNER * 4 + (8 << 20))
    return pl.pallas_call(
        functools.partial(_ssd_kernel, L=L, T=T, n_chunks=n_chunks),
        grid=(b, n_chunks),
        in_specs=[
            pl.BlockSpec((1, T, CONV_DIM), lambda bi, ci: (bi, ci, COL_XBC // CONV_DIM)),
            pl.BlockSpec((1, T, D_INNER), lambda bi, ci: (bi, ci, COL_Z // D_INNER)),
            pl.BlockSpec((1, T, DT_PAD), lambda bi, ci: (bi, ci, 0)),
            pl.BlockSpec((1, SUBLANES, CONV_DIM), lambda bi, ci: (bi, 0, 0)),
            pl.BlockSpec((1, D_INNER, SSM_STATE), lambda bi, ci: (bi, 0, 0)),
            pl.BlockSpec((SUBLANES, CONV_DIM), const),
            pl.BlockSpec((1, CONV_DIM), const),
            pl.BlockSpec((1, DT_PAD), const),
            pl.BlockSpec((1, DT_PAD), const),
            pl.BlockSpec((1, D_INNER), const),
            pl.BlockSpec((1, D_INNER), const),
            pl.BlockSpec((L, L), const),
            pl.BlockSpec((DT_PAD, D_INNER), const),
            pl.BlockSpec((D_INNER, DT_PAD), const),
        ],
        out_specs=[
            pl.BlockSpec((1, T, D_INNER), lambda bi, ci: (bi, ci, 0)),
            pl.BlockSpec((1, D_INNER, SSM_STATE), lambda bi, ci: (bi, 0, 0)),
        ],
        out_shape=[
            jax.ShapeDtypeStruct((b, t_total, D_INNER), BF16),
            jax.ShapeDtypeStruct((b, D_INNER, SSM_STATE), F32),
        ],
        scratch_shapes=[
            pltpu.VMEM((D_INNER, SSM_STATE), F32),
            pltpu.VMEM((L + SUBLANES, CONV_DIM), F32),
            pltpu.VMEM((L, D_INNER), F32),
        ],
        compiler_params=pltpu.CompilerParams(
            dimension_semantics=("arbitrary", "arbitrary"), vmem_limit_bytes=_vmem_limit(vmem)),
        name="ssd",
    )(proj3, proj3, dt3, conv_prev8, ssm_prev, conv_w8, conv_b, dt_bias, a_log, d_skip_x, norm_w,
      tri, expand, expand_t)


FLASH_CHUNK = 4
OWN_GROUP = 4


def _norm_rope_t(x_t, w_col, cos, sin):
    outs = []
    for h in range(2):
        xh = x_t[h * ATT_HEAD_DIM:(h + 1) * ATT_HEAD_DIM, :]
        ms = jnp.mean(xh * xh, axis=0, keepdims=True)
        xn = xh * lax.rsqrt(ms + RMS_EPS) * w_col[h * ATT_HEAD_DIM:(h + 1) * ATT_HEAD_DIM, :]
        x1 = xn[0:ROPE_HALF, :]
        x2 = xn[ROPE_HALF:ROPE_DIM, :]
        outs += [x1 * cos - x2 * sin, x2 * cos + x1 * sin, xn[ROPE_DIM:, :]]
    return jnp.concatenate(outs, axis=0)


def _moba_prompt_kernel(q_ref, k_ref, v_ref, qw_ref, kw_ref, cos_ref, sin_ref, y_ref, kout_ref, vout_ref,
                        qt_ref, qtf_ref, kn_ref, vtc_ref, vtb_ref, km_ref, s_ref, mx_ref, m_ref, l_ref, acc_ref,
                        m0_ref, l0_ref, acc0_ref, *, n_blocks):
    blk = MOBA_BLOCK
    cw = FLASH_CHUNK
    n_chunks = n_blocks // cw
    hd = ATT_HEAD_DIM
    pair_w = 2 * hd
    scale = (ATT_HEAD_DIM ** -0.5) * LOG2_E
    lane_k = lax.broadcasted_iota(jnp.int32, (blk, pair_w), 1)

    for i in range(n_blocks):
        r0 = i * blk
        c0 = (i % cw) * blk
        cos = cos_ref[i]
        sin = sin_ref[i]
        qn = _norm_rope_t(q_ref[0, r0:r0 + blk, :].T, qw_ref[...], cos, sin)
        qtf_ref[i] = qn
        qs = (qn * scale).astype(BF16)
        qt_ref[0, i, 0:hd, :] = qs[0:hd, :]
        qt_ref[1, i, 0:hd, :] = qs[hd:pair_w, :]
        kn_t = _norm_rope_t(k_ref[0, r0:r0 + blk, :].T, kw_ref[...], cos, sin)
        kout_ref[0, :, r0:r0 + blk] = kn_t
        kn = kn_t.T
        km_ref[i:i + 1, :] = jnp.mean(kn, axis=0, keepdims=True)
        onehot = (lane_k == hd + i).astype(F32)
        kn_ref[0, i // cw, c0:c0 + blk, :] = jnp.where(lane_k < hd, kn, onehot).astype(BF16)
        kn_ref[1, i // cw, c0:c0 + blk, :] = jnp.where(lane_k < hd, pltpu.roll(kn, hd, 1), onehot).astype(BF16)
        vt = v_ref[0, r0:r0 + blk, :].T
        vout_ref[0, :, r0:r0 + blk] = vt
        vt = vt.astype(BF16)
        vtb_ref[i] = vt
        vtc_ref[i // cw, :, c0:c0 + blk] = vt

    km = km_ref[...]
    km_lane = lax.broadcasted_iota(jnp.int32, km.shape, 1)
    km_heads = (jnp.where(km_lane < hd, km, 0.0), jnp.where(km_lane < hd, 0.0, km))
    jrow = lax.broadcasted_iota(jnp.int32, (n_blocks, blk), 0)
    k_sel = min(MOBA_TOPK, n_blocks)
    pad_rows = jnp.zeros((hd - n_blocks, blk), BF16)

    for i in range(n_blocks):
        qtf = qtf_ref[i]
        for h in range(2):
            gate = _dot_f32(km_heads[h], qtf)
            valid = jrow < i
            gate = jnp.where(valid, gate, NEG_INF)
            cnt = jnp.zeros((n_blocks, blk), jnp.int32)
            for jp in range(i):
                gj = gate[jp:jp + 1, :]
                better = (gj > gate) | ((gj == gate) & (jp < jrow))
                cnt = cnt + better.astype(jnp.int32)
            sel = valid & (cnt < k_sel)
            bias = jnp.where(sel, 0.0, MASK_BIAS).astype(BF16)
            qt_ref[h, i, hd:pair_w, :] = jnp.concatenate([bias, pad_rows], axis=0)

    key_i = lax.broadcasted_iota(jnp.int32, (blk, blk), 0)
    qry_i = lax.broadcasted_iota(jnp.int32, (blk, blk), 1)
    no_bias = jnp.zeros((hd, blk), BF16)
    def own_scores(i, h):
        c0 = (i % cw) * blk
        return _dot(kn_ref[h, i // cw, c0:c0 + blk, :], jnp.concatenate([qt_ref[h, i, 0:hd, :], no_bias], axis=0))

    tiles = [(i, h) for i in range(n_blocks) for h in range(2)]
    groups = [tiles[g:g + OWN_GROUP] for g in range(0, len(tiles), OWN_GROUP)]
    s_cur = [own_scores(i, h) for i, h in groups[0]]
    for g, grp in enumerate(groups):
        s_nxt = [own_scores(i, h) for i, h in groups[g + 1]] if g + 1 < len(groups) else []
        for (i, h), s_raw in zip(grp, s_cur):
            s = jnp.where(key_i <= qry_i, s_raw, NEG_INF)
            m = jnp.max(s, axis=0, keepdims=True)
            p = jnp.exp2(s - m)
            m0_ref[h, i] = m
            l0_ref[h, i] = jnp.sum(p, axis=0, keepdims=True)
            acc0_ref[h, i] = _dot(vtb_ref[i, h * hd:(h + 1) * hd, :], p.astype(BF16))
        s_cur = s_nxt

    mx_ref[...] = jnp.full(mx_ref.shape, NEG_INF, F32)

    def score_tile(c, buf, h, u, qt):
        t = _dot(kn_ref[h, c, u * blk:(u + 1) * blk, :], qt)
        s_ref[buf, h, u * blk:(u + 1) * blk, :] = t
        mx_ref[buf, h, u:u + 1, :] = jnp.max(t, axis=0, keepdims=True)

    def chunk_scores(c, buf, h, qt):
        for u in range(cw):
            score_tile(c, buf, h, u, qt)

    if n_blocks > 1:
        for h in range(2):
            chunk_scores(0, 0, h, qt_ref[h, 1])

    def attend(i, carry):
        r0 = pl.multiple_of(i * blk, blk)
        qts = [qt_ref[h, i] for h in range(2)]
        for h in range(2):
            m_ref[h] = m0_ref[h, i]
            l_ref[h] = l0_ref[h, i]
            acc_ref[h] = acc0_ref[h, i]

        def region(c, with_next):
            slot = c % 2
            m1s, alphas, l1s, pvs = [], [], [], [None, None]
            for h in range(2):
                m0 = m_ref[h]
                m1 = m0
                for u in range(cw):
                    m1 = jnp.maximum(m1, mx_ref[slot, h, u:u + 1, :])
                m1s.append(m1)
                alphas.append(jnp.exp2(m0 - m1))
                l1s.append(alphas[h] * l_ref[h])
            for u in range(cw):
                for h in range(2):
                    if with_next:
                        score_tile(c + 1, 1 - slot, h, u, qts[h])
                    p = jnp.exp2(s_ref[slot, h, u * blk:(u + 1) * blk, :] - m1s[h])
                    l1s[h] = l1s[h] + jnp.sum(p, axis=0, keepdims=True)
                    d = _dot(vtc_ref[c, h * hd:(h + 1) * hd, u * blk:(u + 1) * blk], p.astype(BF16))
                    pvs[h] = d if pvs[h] is None else pvs[h] + d
            for h in range(2):
                m_ref[h] = m1s[h]
                l_ref[h] = l1s[h]
                acc_ref[h] = alphas[h] * acc_ref[h] + pvs[h]

        for c in range(n_chunks):
            if c + 1 < n_chunks:
                pl.when((c + 1) * cw < i)(functools.partial(region, c, True))
                pl.when((c * cw < i) & ((c + 1) * cw >= i))(functools.partial(region, c, False))
            else:
                pl.when(c * cw < i)(functools.partial(region, c, False))

        o_t = jnp.concatenate([acc_ref[0] / l_ref[0], acc_ref[1] / l_ref[1]], axis=0)
        y_ref[0, pl.ds(r0, blk), :] = o_t.T.astype(BF16)
        nxt = jnp.minimum(i + 1, n_blocks - 1)
        for h in range(2):
            chunk_scores(0, 0, h, qt_ref[h, nxt])
        return carry

    lax.fori_loop(0, n_blocks, attend, 0)


def _moba_prompt(proj3, v3, q_norm_w, k_norm_w, cos_t, sin_t):
    b, s, _ = proj3.shape
    n_blocks = s // MOBA_BLOCK
    assert n_blocks % FLASH_CHUNK == 0 and n_blocks <= ATT_HEAD_DIM
    n_chunks = n_blocks // FLASH_CHUNK
    n_pairs = ATT_HEADS // 2
    pw = 2 * ATT_HEAD_DIM
    cblk = FLASH_CHUNK * MOBA_BLOCK
    qw = jnp.tile(q_norm_w, 2).reshape(pw, 1)
    kw = jnp.tile(k_norm_w, 2).reshape(pw, 1)
    const3 = lambda bi, p: (0, 0, 0)
    vmem = (2 * 3 * s * pw * 4 + 2 * s * pw * 2 + 4 * s * pw * 4 + 7 * s * pw * 2 + s * pw * 4
            + 4 * cblk * MOBA_BLOCK * 4 + (16 << 20))
    y_att, k_t, v_t = pl.pallas_call(
        functools.partial(_moba_prompt_kernel, n_blocks=n_blocks),
        grid=(b, n_pairs),
        in_specs=[
            pl.BlockSpec((1, s, pw), lambda bi, p: (bi, 0, COL_Q // pw + p)),
            pl.BlockSpec((1, s, pw), lambda bi, p: (bi, 0, COL_K // pw + p)),
            pl.BlockSpec((1, s, pw), lambda bi, p: (bi, 0, p)),
            pl.BlockSpec((pw, 1), lambda bi, p: (0, 0)),
            pl.BlockSpec((pw, 1), lambda bi, p: (0, 0)),
            pl.BlockSpec((n_blocks, ROPE_HALF, MOBA_BLOCK), const3),
            pl.BlockSpec((n_blocks, ROPE_HALF, MOBA_BLOCK), const3),
        ],
        out_specs=[
            pl.BlockSpec((1, s, pw), lambda bi, p: (bi, 0, p)),
            pl.BlockSpec((1, pw, s), lambda bi, p: (bi, p, 0)),
            pl.BlockSpec((1, pw, s), lambda bi, p: (bi, p, 0)),
        ],
        out_shape=[
            jax.ShapeDtypeStruct((b, s, D_ATT), BF16),
            jax.ShapeDtypeStruct((b, D_ATT, s), F32),
            jax.ShapeDtypeStruct((b, D_ATT, s), F32),
        ],
        scratch_shapes=[
            pltpu.VMEM((2, n_blocks, pw, MOBA_BLOCK), BF16),
            pltpu.VMEM((n_blocks, pw, MOBA_BLOCK), F32),
            pltpu.VMEM((2, n_chunks, cblk, pw), BF16),
            pltpu.VMEM((n_chunks, pw, cblk), BF16),
            pltpu.VMEM((n_blocks, pw, MOBA_BLOCK), BF16),
            pltpu.VMEM((n_blocks, pw), F32),
            pltpu.VMEM((2, 2, cblk, MOBA_BLOCK), F32),
            pltpu.VMEM((2, 2, SUBLANES, MOBA_BLOCK), F32),
            pltpu.VMEM((2, 1, MOBA_BLOCK), F32),
            pltpu.VMEM((2, 1, MOBA_BLOCK), F32),
            pltpu.VMEM((2, ATT_HEAD_DIM, MOBA_BLOCK), F32),
            pltpu.VMEM((2, n_blocks, 1, MOBA_BLOCK), F32),
            pltpu.VMEM((2, n_blocks, 1, MOBA_BLOCK), F32),
            pltpu.VMEM((2, n_blocks, ATT_HEAD_DIM, MOBA_BLOCK), F32),
        ],
        compiler_params=pltpu.CompilerParams(
            dimension_semantics=("arbitrary", "arbitrary"), vmem_limit_bytes=_vmem_limit(vmem)),
        name="moba_prompt",
    )(proj3, proj3, v3, qw, kw, cos_t, sin_t)
    to_heads = lambda t: t.reshape(b, ATT_HEADS, ATT_HEAD_DIM, s).transpose(0, 3, 1, 2)
    return y_att, to_heads(k_t), to_heads(v_t)


def _seg_sum(x, seg):
    w = x.shape[-1]
    lane = lax.broadcasted_iota(jnp.int32, x.shape, x.ndim - 1)
    s = 1
    while s < seg:
        x = x + jnp.where((lane & s) != 0, pltpu.roll(x, s, x.ndim - 1), pltpu.roll(x, w - s, x.ndim - 1))
        s *= 2
    return x


def _norm_rope_rows(x, w_row, cosf, sin_lo, sin_hi):
    ms = _seg_sum(x * x, ATT_HEAD_DIM) * (1.0 / ATT_HEAD_DIM)
    xn = x * lax.rsqrt(ms + RMS_EPS) * w_row
    width = x.shape[-1]
    return (xn * cosf + pltpu.roll(xn, width - ROPE_HALF, 1) * sin_lo + pltpu.roll(xn, ROPE_HALF, 1) * sin_hi)


def _moba_sample_kernel(pt_ref, q_ref, k_ref, v_ref, qw_ref, kw_ref, cosf_ref, slo_ref, shi_ref, hmask_ref,
                        *rest, T, pages_per_step, n_steps, n_blocks):
    kp = rest[:pages_per_step]
    vp = rest[pages_per_step:2 * pages_per_step]
    y_ref, kout_ref = rest[2 * pages_per_step:2 * pages_per_step + 2]
    qbd_ref, knp_ref, vnp_ref, ms_ref, ls_ref, gs_ref, o_ref = rest[2 * pages_per_step + 2:]
    step = pl.program_id(1)
    rows = ATT_HEADS * T
    scale = ATT_HEAD_DIM ** -0.5
    hmask = hmask_ref[...]

    def fold(o_all):
        o_all = o_all * hmask
        acc = o_all[:, 0:LANES]
        for u in range(1, D_ATT // LANES):
            acc = acc + o_all[:, u * LANES:(u + 1) * LANES]
        return acc

    @pl.when(step == 0)
    def _():
        qn = _norm_rope_rows(q_ref[0], qw_ref[...], cosf_ref[...], slo_ref[...], shi_ref[...])
        kn = _norm_rope_rows(k_ref[0], kw_ref[...], cosf_ref[...], slo_ref[...], shi_ref[...])
        kout_ref[0] = kn
        q_rows = jnp.concatenate([qn * scale] * ATT_HEADS, axis=0) * hmask
        q_hi, q_lo = _split2(q_rows)
        qbd_ref[0:rows, :] = q_hi
        qbd_ref[rows:2 * rows, :] = q_lo
        pad = jnp.zeros((rows - T, D_ATT), F32)
        knp_ref[...] = jnp.concatenate([kn, pad], axis=0).astype(BF16)
        vnp_ref[...] = jnp.concatenate([v_ref[0], pad], axis=0).astype(BF16)
        ms_ref[...] = jnp.full((rows, LANES), NEG_INF, F32)
        gs_ref[...] = jnp.full((rows, LANES), NEG_INF, F32)
        ls_ref[...] = jnp.zeros((rows, LANES), F32)

    lane = lax.broadcasted_iota(jnp.int32, (rows, LANES), 1)
    qbd = qbd_ref[...]
    blocks_per_step = pages_per_step // PAGES_PER_BLOCK
    for jj in range(blocks_per_step):
        j = step * blocks_per_step + jj
        kt = jnp.concatenate([kp[jj * PAGES_PER_BLOCK + u][0, 0].reshape(D_ATT, PAGE_SIZE)
                              for u in range(PAGES_PER_BLOCK)], axis=1).astype(BF16)
        vt = jnp.concatenate([vp[jj * PAGES_PER_BLOCK + u][0, 0].reshape(D_ATT, PAGE_SIZE)
                              for u in range(PAGES_PER_BLOCK)], axis=1).astype(BF16)
        s2 = _dot(qbd, kt)
        s = s2[0:rows, :] + s2[rows:2 * rows, :]
        g_j = jnp.mean(s, axis=1, keepdims=True)
        m_j = jnp.max(s, axis=1, keepdims=True)
        p = jnp.exp(s - m_j)
        l_j = jnp.sum(p, axis=1, keepdims=True)
        o_ref[j] = fold(_dot_nt(p.astype(BF16), vt))
        here = lane == j
        ms_ref[...] = jnp.where(here, m_j, ms_ref[...])
        ls_ref[...] = jnp.where(here, l_j, ls_ref[...])
        gs_ref[...] = jnp.where(here, g_j, gs_ref[...])

    @pl.when(step == n_steps - 1)
    def _():
        r = lax.broadcasted_iota(jnp.int32, (rows, LANES), 0)
        s2 = _dot_nt(qbd, knp_ref[...])
        s_own = s2[0:rows, :] + s2[rows:2 * rows, :]
        s_own = jnp.where((lane < T) & (lane <= (r % T)), s_own, NEG_INF)
        m_o = jnp.max(s_own, axis=1, keepdims=True)
        p_o = jnp.exp(s_own - m_o)
        l_o = jnp.sum(p_o, axis=1, keepdims=True)
        o_o = fold(_dot(p_o.astype(BF16), vnp_ref[...]))

        gs = gs_ref[...]
        cnt = jnp.zeros((rows, LANES), jnp.int32)
        for jp in range(n_blocks):
            gj = gs[:, jp:jp + 1]
            better = (gj > gs) | ((gj == gs) & (jp < lane))
            cnt = cnt + better.astype(jnp.int32)
        sel = (lane < n_blocks) & (cnt < min(MOBA_TOPK, n_blocks))
        ms = ms_ref[...]
        m_tot = jnp.maximum(jnp.max(jnp.where(sel, ms, NEG_INF), axis=1, keepdims=True), m_o)
        w = jnp.where(sel, jnp.exp(ms - m_tot), 0.0)
        w_o = jnp.exp(m_o - m_tot)
        l_tot = jnp.sum(w * ls_ref[...], axis=1, keepdims=True) + w_o * l_o
        acc = w_o * o_o
        for jp in range(n_blocks):
            acc = acc + w[:, jp:jp + 1] * o_ref[jp]
        out = acc / l_tot
        even_head = lax.broadcasted_iota(jnp.int32, (T, LANES), 1) < ATT_HEAD_DIM
        y_ref[0] = jnp.concatenate(
            [jnp.where(even_head, out[2 * hp * T:(2 * hp + 1) * T, :], out[(2 * hp + 1) * T:(2 * hp + 2) * T, :])
             for hp in range(ATT_HEADS // 2)], axis=1).astype(BF16)


def _moba_sample(q3, k3, v3, cache_k, cache_v, page_table, layer, q_norm_w, k_norm_w, cosf, sin_lo, sin_hi):
    nseq, T, _ = v3.shape
    n_pages = page_table.shape[1]
    assert T == SUBLANES and n_pages % PAGES_PER_BLOCK == 0
    n_blocks = n_pages // PAGES_PER_BLOCK
    assert n_blocks <= LANES
    pages_per_step = 16 if n_pages % 16 == 0 else PAGES_PER_BLOCK
    assert n_pages % pages_per_step == 0
    n_steps = n_pages // pages_per_step
    rows = ATT_HEADS * T
    hmask = ((jnp.arange(rows)[:, None] // T) == (jnp.arange(D_ATT)[None, :] // ATT_HEAD_DIM)).astype(F32)
    qw = jnp.tile(q_norm_w, ATT_HEADS).reshape(1, D_ATT)
    kw = jnp.tile(k_norm_w, ATT_HEADS).reshape(1, D_ATT)
    ck_t = jnp.transpose(cache_k, (0, 1, 3, 4, 2))
    cv_t = jnp.transpose(cache_v, (0, 1, 3, 4, 2))

    tok = lambda b, s, pt: (b, 0, 0)
    const = lambda b, s, pt: (0, 0)
    page_shape = (1, 1, ATT_HEADS, ATT_HEAD_DIM, PAGE_SIZE)

    def page_spec(u):
        return pl.BlockSpec(page_shape, lambda b, s, pt, u=u: (layer, pt[b, s * pages_per_step + u], 0, 0, 0))

    in_specs = [
        pl.BlockSpec((1, T, D_ATT), lambda b, s, pt: (b, 0, COL_Q // D_ATT)),
        pl.BlockSpec((1, T, D_ATT), lambda b, s, pt: (b, 0, COL_K // D_ATT)),
        pl.BlockSpec((1, T, D_ATT), tok),
        pl.BlockSpec((1, D_ATT), const),
        pl.BlockSpec((1, D_ATT), const),
        pl.BlockSpec((T, D_ATT), const),
        pl.BlockSpec((T, D_ATT), const),
        pl.BlockSpec((T, D_ATT), const),
        pl.BlockSpec((rows, D_ATT), const),
    ] + [page_spec(u) for u in range(pages_per_step)] * 2
    page_vmem = PAGE_SIZE * D_ATT * 4
    vmem = 2 * 2 * pages_per_step * page_vmem + n_blocks * rows * LANES * 4 + (24 << 20)
    return pl.pallas_call(
        functools.partial(_moba_sample_kernel, T=T, pages_per_step=pages_per_step, n_steps=n_steps,
                          n_blocks=n_blocks),
        grid_spec=pltpu.PrefetchScalarGridSpec(
            num_scalar_prefetch=1,
            grid=(nseq, n_steps),
            in_specs=in_specs,
            out_specs=[pl.BlockSpec((1, T, D_ATT), tok), pl.BlockSpec((1, T, D_ATT), tok)],
            scratch_shapes=[
                pltpu.VMEM((2 * rows, D_ATT), BF16),
                pltpu.VMEM((rows, D_ATT), BF16),
                pltpu.VMEM((rows, D_ATT), BF16),
                pltpu.VMEM((rows, LANES), F32),
                pltpu.VMEM((rows, LANES), F32),
                pltpu.VMEM((rows, LANES), F32),
                pltpu.VMEM((n_blocks, rows, LANES), F32),
            ],
        ),
        out_shape=[
            jax.ShapeDtypeStruct((nseq, T, D_ATT), BF16),
            jax.ShapeDtypeStruct((nseq, T, D_ATT), F32),
        ],
        compiler_params=pltpu.CompilerParams(
            dimension_semantics=("arbitrary", "arbitrary"), vmem_limit_bytes=_vmem_limit(vmem)),
        name="moba_sample",
    )(page_table, q3, k3, v3, qw, kw, cosf, sin_lo, sin_hi, hmask,
      *([ck_t] * pages_per_step), *([cv_t] * pages_per_step))


def _merge_kernel(x_ref, yssm_ref, yatt_ref, g_ref, wssm_ref, watt_ref, wout_ref, n2w_ref, x1_ref, h2_ref):
    u_ssm = _dot(yssm_ref[...], wssm_ref[...])
    u_att = _dot(yatt_ref[...], watt_ref[...])
    g = _sigmoid(g_ref[...])
    merged = (g[:, :D_MODEL] * u_ssm + g[:, D_MODEL:] * u_att).astype(BF16)
    x1 = x_ref[...] + _dot(merged, wout_ref[...])
    x1_ref[...] = x1
    ms = jnp.mean(x1 * x1, axis=-1, keepdims=True)
    h2_ref[...] = (x1 * lax.rsqrt(ms + RMS_EPS) * n2w_ref[...]).astype(BF16)


def _merge(x2d, y_ssm, y_att, proj, w_ssm, w_att, w_out, norm2_w, tm):
    n = x2d.shape[0]
    const = lambda i: (0, 0)
    vmem = (2 * tm * (D_MODEL * 4 + D_INNER * 2 + D_ATT * 2 + 2 * D_MODEL * 4 + D_MODEL * 4 + D_MODEL * 2)
            + 2 * 2 * (D_INNER + D_ATT + D_MODEL) * D_MODEL + 8 * tm * D_MODEL * 4 + (4 << 20))
    return pl.pallas_call(
        _merge_kernel,
        grid=(n // tm,),
        in_specs=[
            pl.BlockSpec((tm, D_MODEL), lambda i: (i, 0)),
            pl.BlockSpec((tm, D_INNER), lambda i: (i, 0)),
            pl.BlockSpec((tm, D_ATT), lambda i: (i, 0)),
            pl.BlockSpec((tm, 2 * D_MODEL), lambda i: (i, COL_G // (2 * D_MODEL))),
            pl.BlockSpec((D_INNER, D_MODEL), const),
            pl.BlockSpec((D_ATT, D_MODEL), const),
            pl.BlockSpec((D_MODEL, D_MODEL), const),
            pl.BlockSpec((1, D_MODEL), const),
        ],
        out_specs=[pl.BlockSpec((tm, D_MODEL), lambda i: (i, 0)), pl.BlockSpec((tm, D_MODEL), lambda i: (i, 0))],
        out_shape=[jax.ShapeDtypeStruct((n, D_MODEL), F32), jax.ShapeDtypeStruct((n, D_MODEL), BF16)],
        compiler_params=pltpu.CompilerParams(
            dimension_semantics=("arbitrary",), vmem_limit_bytes=_vmem_limit(vmem)),
        name="merge",
    )(x2d, y_ssm, y_att, proj, w_ssm, w_att, w_out, norm2_w)


def _ffn_kernel(h2_ref, x1_ref, wg_ref, wu_ref, wd_ref, out_ref):
    f = pl.program_id(1)
    h2 = h2_ref[...]
    gate = _dot(h2, wg_ref[...])
    up = _dot(h2, wu_ref[...])
    act = (gate * _sigmoid(gate) * up).astype(BF16)
    part = _dot(act, wd_ref[...])

    @pl.when(f == 0)
    def _():
        out_ref[...] = x1_ref[...] + part

    @pl.when(f != 0)
    def _():
        out_ref[...] = out_ref[...] + part


def _ffn(h2, x1, w_gate, w_up, w_down, tm, tf):
    n = h2.shape[0]
    vmem = (2 * tm * D_MODEL * (2 + 4 + 4) + 2 * 3 * D_MODEL * tf * 2 + 6 * tm * tf * 4 + (4 << 20))
    return pl.pallas_call(
        _ffn_kernel,
        grid=(n // tm, D_FF // tf),
        in_specs=[
            pl.BlockSpec((tm, D_MODEL), lambda i, f: (i, 0)),
            pl.BlockSpec((tm, D_MODEL), lambda i, f: (i, 0)),
            pl.BlockSpec((D_MODEL, tf), lambda i, f: (0, f)),
            pl.BlockSpec((D_MODEL, tf), lambda i, f: (0, f)),
            pl.BlockSpec((tf, D_MODEL), lambda i, f: (f, 0)),
        ],
        out_specs=pl.BlockSpec((tm, D_MODEL), lambda i, f: (i, 0)),
        out_shape=jax.ShapeDtypeStruct((n, D_MODEL), F32),
        compiler_params=pltpu.CompilerParams(
            dimension_semantics=("arbitrary", "arbitrary"), vmem_limit_bytes=_vmem_limit(vmem)),
        name="ffn",
    )(h2, x1, w_gate, w_up, w_down)


def _rope_angles(pos):
    inv_freq = ROPE_THETA ** (-(jnp.arange(ROPE_HALF, dtype=F32) * 2.0 / ROPE_DIM))
    ang = pos.astype(F32)[:, None] * inv_freq[None, :]
    return jnp.cos(ang), jnp.sin(ang)


def _rope_tables_t(pos, n_blocks):
    cos, sin = _rope_angles(pos)
    to_blocks = lambda t: t.T.reshape(ROPE_HALF, n_blocks, MOBA_BLOCK).transpose(1, 0, 2)
    return to_blocks(cos), to_blocks(sin)


def _rope_tables_rows(pos):
    cos, sin = _rope_angles(pos)
    d = jnp.arange(D_ATT) % ATT_HEAD_DIM
    idx = d % ROPE_HALF
    cosf = jnp.where(d[None, :] < ROPE_DIM, cos[:, idx], 1.0)
    sin_lo = jnp.where(d[None, :] < ROPE_HALF, -sin[:, idx], 0.0)
    sin_hi = jnp.where((d[None, :] >= ROPE_HALF) & (d[None, :] < ROPE_DIM), sin[:, idx], 0.0)
    return cosf.astype(F32), sin_lo.astype(F32), sin_hi.astype(F32)


def _layer_weights(l, norm1_w, w_in, conv_w, conv_b, dt_bias, a_log, d_skip, ssm_norm_w, q_norm_w, k_norm_w,
                   w_ssm_branch, w_att_branch, w_out, norm2_w, w_ffn_in, w_ffn_out):
    w = w_in[l]
    off_xbc = D_INNER
    off_dt = off_xbc + CONV_DIM
    off_q = off_dt + SSM_HEADS
    off_k = off_q + D_ATT
    off_v = off_k + D_ATT
    off_g = off_v + D_ATT
    w_main = jnp.concatenate([w[:, off_xbc:off_dt], w[:, off_q:off_k], w[:, 0:off_xbc], w[:, off_g:],
                              w[:, off_k:off_v], w[:, off_v:off_g]], axis=1).astype(BF16)
    w_dt = jnp.pad(w[:, off_dt:off_q], ((0, 0), (0, DT_PAD - SSM_HEADS)))
    wdt_hi = w_dt.astype(BF16)
    wdt_lo = (w_dt - wdt_hi.astype(F32)).astype(BF16)
    pad_h = (0, DT_PAD - SSM_HEADS)
    return dict(
        norm1_w=norm1_w[l].reshape(1, D_MODEL), w_main=w_main, wdt_hi=wdt_hi, wdt_lo=wdt_lo,
        conv_w8=jnp.pad(conv_w[l], ((0, SUBLANES - CONV_WIDTH), (0, 0))), conv_b=conv_b[l].reshape(1, CONV_DIM),
        dt_bias=jnp.pad(dt_bias[l], pad_h).reshape(1, DT_PAD), a_log=jnp.pad(a_log[l], pad_h).reshape(1, DT_PAD),
        d_skip_x=jnp.repeat(d_skip[l], SSM_HEAD_DIM).reshape(1, D_INNER),
        ssm_norm_w=ssm_norm_w[l].reshape(1, D_INNER), q_norm_w=q_norm_w[l], k_norm_w=k_norm_w[l],
        w_ssm=w_ssm_branch[l].astype(BF16), w_att=w_att_branch[l].astype(BF16), w_out=w_out[l].astype(BF16),
        norm2_w=norm2_w[l].reshape(1, D_MODEL),
        w_gate=w_ffn_in[l][:, :D_FF].astype(BF16), w_up=w_ffn_in[l][:, D_FF:].astype(BF16),
        w_down=w_ffn_out[l].astype(BF16),
    )


def _trunk(x, conv_prev, ssm_prev, wts, attend, ssd_chunk, tm_proj, tm, tf):
    b, t, _ = x.shape
    n = b * t
    x2d = x.reshape(n, D_MODEL)
    proj, v, dt = _in_proj(x2d, wts["norm1_w"], wts["w_main"], wts["wdt_hi"], wts["wdt_lo"], tm_proj)
    proj3 = proj.reshape(b, t, PROJ_COLS)
    v3 = v.reshape(b, t, D_ATT)
    conv_prev8 = jnp.pad(conv_prev, ((0, 0), (SUBLANES - (CONV_WIDTH - 1), 0), (0, 0)))
    L, T = ssd_chunk
    y_ssm, ssm_new = _ssd(proj3, dt.reshape(b, t, DT_PAD), conv_prev8, ssm_prev.reshape(b, D_INNER, SSM_STATE),
                          wts["conv_w8"], wts["conv_b"], wts["dt_bias"], wts["a_log"], wts["d_skip_x"],
                          wts["ssm_norm_w"], L, T)
    y_att, k_heads, v_heads = attend(proj3, v3)
    x1, h2 = _merge(x2d, y_ssm.reshape(n, D_INNER), y_att.reshape(n, D_ATT), proj, wts["w_ssm"], wts["w_att"],
                    wts["w_out"], wts["norm2_w"], tm)
    out = _ffn(h2, x1, wts["w_gate"], wts["w_up"], wts["w_down"], tm, tf)
    conv_new = proj3[:, t - (CONV_WIDTH - 1):, COL_XBC:COL_XBC + CONV_DIM]
    return (out.reshape(b, t, D_MODEL), k_heads, v_heads, conv_new,
            ssm_new.reshape(b, SSM_HEADS, SSM_HEAD_DIM, SSM_STATE))


def kernel(x_prompt, x_sample, cache_k, cache_v, page_table, state_conv, state_ssm, norm1_w, w_in, conv_w, conv_b, dt_bias, a_log, d_skip, ssm_norm_w, q_norm_w, k_norm_w, w_ssm_branch, w_att_branch, w_out, norm2_w, w_ffn_in, w_ffn_out):
    depth = w_in.shape[0]
    bp, sp, _ = x_prompt.shape
    bs, ts, _ = x_sample.shape
    past_len = page_table.shape[1] * PAGE_SIZE
    assert sp % MOBA_BLOCK == 0 and sp % SSD_CHUNK == 0 and ts <= SSD_CHUNK
    cos_t, sin_t = _rope_tables_t(jnp.arange(sp), sp // MOBA_BLOCK)
    cosf, sin_lo, sin_hi = _rope_tables_rows(past_len + jnp.arange(ts))
    xp, xs = x_prompt, x_sample
    outs = [[] for _ in range(8)]
    for l in range(depth):
        wts = _layer_weights(l, norm1_w, w_in, conv_w, conv_b, dt_bias, a_log, d_skip, ssm_norm_w, q_norm_w,
                             k_norm_w, w_ssm_branch, w_att_branch, w_out, norm2_w, w_ffn_in, w_ffn_out)
        attend_p = lambda proj3, v3: _moba_prompt(proj3, v3, wts["q_norm_w"], wts["k_norm_w"], cos_t, sin_t)
        conv0 = jnp.zeros((bp, CONV_WIDTH - 1, CONV_DIM), F32)
        ssm0 = jnp.zeros((bp, SSM_HEADS, SSM_HEAD_DIM, SSM_STATE), F32)
        xp, kp, vp, cp, ssp = _trunk(xp, conv0, ssm0, wts, attend_p, (SSD_CHUNK, SSD_CHUNK), min(1024, bp * sp),
                                     512, D_FF // 2)

        def attend_s(proj3, v3):
            y_att, k3 = _moba_sample(proj3, proj3, v3, cache_k, cache_v, page_table, l, wts["q_norm_w"],
                                     wts["k_norm_w"], cosf, sin_lo, sin_hi)
            heads = (bs, ts, ATT_HEADS, ATT_HEAD_DIM)
            return y_att, k3.reshape(heads), v3.reshape(heads)

        xs, ksn, vsn, cs, sss = _trunk(xs, state_conv[l], state_ssm[l], wts, attend_s, (LANES, ts), bs * ts,
                                       bs * ts, D_FF // 2)
        for lst, val in zip(outs, (kp, vp, ksn, vsn, cp, cs, ssp, sss)):
            lst.append(val)
    return (xp, xs) + tuple(jnp.stack(o) for o in outs)
```
